```python
import math
import jax, jax.numpy as jnp
from jax import lax
import numpy as np

D_MODEL = 1024
BATCH = 8
SEQ = 2048
DEPTH = 4

CHUNK = 64
N_META = 16
N_MIXERS = 2
N_FOX = (DEPTH + 1) // 2
N_RWKV = DEPTH // 2
D_INNER = D_MODEL
FOX_HEAD_DIM = 64
FOX_HEADS = D_INNER // FOX_HEAD_DIM
Q_BLOCK = 128
FOX_IN = 4 * D_INNER + FOX_HEADS
RWKV_HEAD = 64
RWKV_HEADS = D_INNER // RWKV_HEAD
LORA_W = 64
LORA_A = 64
RWKV_IN = 4 * D_INNER + LORA_W + LORA_A
NORM_EPS = 1e-6
GN_EPS = 64e-5
DECAY_SCALE = math.exp(-0.5)

kernel_name = 'fox_rwkv7_meta_hybrid'


def _rmsnorm(x, g):
    xf = x.astype(jnp.float32)
    y = xf * lax.rsqrt(jnp.mean(xf * xf, axis=-1, keepdims=True) + NORM_EPS)
    return (y * g.astype(jnp.float32)).astype(x.dtype)


def _heads(t, n_heads, head_dim):
    B, L, _ = t.shape
    return t.reshape(B, L, n_heads, head_dim).transpose(0, 2, 1, 3)


def _fox_mixer(u, w_in, b_f, w_out):
    B, L, _ = u.shape
    p = u @ w_in
    q, k, v, gate, f_logit = jnp.split(p, [D_INNER, 2 * D_INNER, 3 * D_INNER, 4 * D_INNER], axis=-1)
    q = _heads(q, FOX_HEADS, FOX_HEAD_DIM) * (FOX_HEAD_DIM ** -0.5)
    k = _heads(k, FOX_HEADS, FOX_HEAD_DIM)
    v = _heads(v, FOX_HEADS, FOX_HEAD_DIM)
    log_f = jax.nn.log_sigmoid((f_logit + b_f).astype(jnp.float32))
    cum = jnp.cumsum(log_f, axis=1).transpose(0, 2, 1)
    outs = []
    for start in range(0, L, Q_BLOCK):
        stop = min(start + Q_BLOCK, L)
        logits = jnp.einsum('bhqd,bhkd->bhqk', q[:, :, start:stop], k[:, :, :stop]).astype(jnp.float32)
        logits = logits + cum[:, :, start:stop, None] - cum[:, :, None, :stop]
        causal = jnp.arange(stop)[None, :] <= jnp.arange(start, stop)[:, None]
        logits = jnp.where(causal, logits, -jnp.inf)
        prob = jax.nn.softmax(logits, axis=-1).astype(v.dtype)
        outs.append(jnp.einsum('bhqk,bhkd->bhqd', prob, v[:, :, :stop]))
    o = jnp.concatenate(outs, axis=2).transpose(0, 2, 1, 3).reshape(B, L, D_INNER)
    return (o * jax.nn.silu(gate)) @ w_out


def _wkv7_scan(r, decay, k, v, a_vec, b_vec):
    B, L, H, N = r.shape

    def step(S, inp):
        r_t, w_t, k_t, v_t, a_t, b_t = inp
        sa = jnp.einsum('bhvk,bhk->bhv', S, a_t)
        S = S * w_t[:, :, None, :] + sa[..., None] * b_t[:, :, None, :] + v_t[..., None] * k_t[:, :, None, :]
        return S, jnp.einsum('bhvk,bhk->bhv', S, r_t)

    seq = tuple(jnp.moveaxis(t, 1, 0) for t in (r, decay, k, v, a_vec, b_vec))
    S0 = jnp.zeros((B, H, N, N), jnp.float32)
    _, y = lax.scan(step, S0, seq)
    return jnp.moveaxis(y, 0, 1)


def _rwkv7_mixer(u, w_in, mu, w0, w_up, a0, a_up, k_k, k_a, r_k, ln_w, ln_b, w_out):
    B, L, _ = u.shape
    f32 = jnp.float32
    p = u @ w_in
    p_prev = jnp.pad(p, ((0, 0), (1, 0), (0, 0)))[:, :L]
    p = p + (p_prev - p) * mu
    r, k, v, gate, wd, ad = jnp.split(
        p, [D_INNER, 2 * D_INNER, 3 * D_INNER, 4 * D_INNER, 4 * D_INNER + LORA_W], axis=-1)
    w_log = (w0 + jnp.tanh(wd) @ w_up).astype(f32)
    decay = jnp.exp(-DECAY_SCALE * jax.nn.sigmoid(w_log))
    a = jax.nn.sigmoid((a0 + ad @ a_up).astype(f32))
    r = r.astype(f32)
    k = k.astype(f32)
    v = v.astype(f32)
    hs = lambda t: t.reshape(B, L, RWKV_HEADS, RWKV_HEAD)
    kk = hs(k * k_k.astype(f32))
    kk = kk / jnp.maximum(jnp.linalg.norm(kk, axis=-1, keepdims=True), 1e-12)
    k = k * (1.0 + (a - 1.0) * k_a.astype(f32))
    r_h, k_h, v_h = hs(r), hs(k), hs(v)
    y = _wkv7_scan(r_h, hs(decay), k_h, v_h, -kk, kk * hs(a))
    mean = jnp.mean(y, axis=-1, keepdims=True)
    var = jnp.mean(jnp.square(y - mean), axis=-1, keepdims=True)
    y = (y - mean) * lax.rsqrt(var + GN_EPS)
    y = y * ln_w.astype(f32).reshape(RWKV_HEADS, RWKV_HEAD) + ln_b.astype(f32).reshape(RWKV_HEADS, RWKV_HEAD)
    bonus = jnp.sum(r_h * k_h * r_k.astype(f32), axis=-1, keepdims=True) * v_h
    y = (y + bonus).reshape(B, L, D_INNER).astype(u.dtype)
    return (y * jax.nn.silu(gate)) @ w_out


def setup_inputs(seed: int = 0) -> dict:
    key = jax.random.key(seed)
    ks = jax.random.split(key, 20)
    f32 = jnp.float32
    D = D_MODEL
    nrm = lambda kk, shape, s: jax.random.normal(kk, shape, f32) * s
    return {
        'x': nrm(ks[0], (BATCH, SEQ, D), 1.0),
        'meta_tokens': nrm(ks[1], (N_META, D), 1.0),
        'norm_pre': 1.0 + nrm(ks[2], (DEPTH, D), 0.02),
        'norm_post': 1.0 + nrm(ks[3], (DEPTH, D), 0.02),
        'fox_w_in': nrm(ks[4], (N_FOX, D, FOX_IN), D ** -0.5),
        'fox_b_f': jax.random.uniform(ks[5], (N_FOX, FOX_HEADS), f32, 1.0, 5.0),
        'fox_w_out': nrm(ks[6], (N_FOX, D_INNER, D), D_INNER ** -0.5),
        'rwkv_w_in': nrm(ks[7], (N_RWKV, D, RWKV_IN), D ** -0.5),
        'rwkv_mu': jax.random.uniform(ks[8], (N_RWKV, RWKV_IN), f32, 0.0, 1.0),
        'rwkv_w0': -0.5 + nrm(ks[9], (N_RWKV, D_INNER), 0.5),
        'rwkv_w_up': nrm(ks[10], (N_RWKV, LORA_W, D_INNER), 0.5 * LORA_W ** -0.5),
        'rwkv_a0': nrm(ks[11], (N_RWKV, D_INNER), 0.1),
        'rwkv_a_up': nrm(ks[12], (N_RWKV, LORA_A, D_INNER), 0.5 * LORA_A ** -0.5),
        'rwkv_k_k': 0.85 + nrm(ks[13], (N_RWKV, D_INNER), 0.05),
        'rwkv_k_a': 1.0 + nrm(ks[14], (N_RWKV, D_INNER), 0.05),
        'rwkv_r_k': nrm(ks[15], (N_RWKV, RWKV_HEADS, RWKV_HEAD), 0.1),
        'rwkv_ln_w': 1.0 + nrm(ks[16], (N_RWKV, D_INNER), 0.02),
        'rwkv_ln_b': nrm(ks[17], (N_RWKV, D_INNER), 0.02),
        'rwkv_w_out': nrm(ks[18], (N_RWKV, D_INNER, D), D_INNER ** -0.5),
    }


def reference(x, meta_tokens, norm_pre, norm_post, fox_w_in, fox_b_f, fox_w_out,
              rwkv_w_in, rwkv_mu, rwkv_w0, rwkv_w_up, rwkv_a0, rwkv_a_up, rwkv_k_k,
              rwkv_k_a, rwkv_r_k, rwkv_ln_w, rwkv_ln_b, rwkv_w_out):
    B = x.shape[0]
    meta = jnp.broadcast_to(meta_tokens[None].astype(x.dtype), (B, N_META, D_MODEL))
    h = jnp.concatenate([meta, x], axis=1)
    for i in range(DEPTH):
        j = i // N_MIXERS
        u = _rmsnorm(h, norm_pre[i])
        if i % N_MIXERS == 0:
            m = _fox_mixer(u, fox_w_in[j], fox_b_f[j], fox_w_out[j])
        else:
            m = _rwkv7_mixer(u, rwkv_w_in[j], rwkv_mu[j], rwkv_w0[j], rwkv_w_up[j], rwkv_a0[j],
                             rwkv_a_up[j], rwkv_k_k[j], rwkv_k_a[j], rwkv_r_k[j], rwkv_ln_w[j],
                             rwkv_ln_b[j], rwkv_w_out[j])
        h = h + _rmsnorm(m, norm_post[i])
    return h[:, N_META:]
```

```python
import functools
import math

import jax
import jax.numpy as jnp
from jax import lax
from jax.experimental import pallas as pl
from jax.experimental.pallas import tpu as pltpu

F32 = jnp.float32
BF16 = jnp.bfloat16

N_META = 16
HEAD = 64
LANES = 128
BLK = 128
CHUNK = 64
ROW_TILE = 512
COL_TILE = 512
NORM_EPS = 1e-6
GN_EPS = 64e-5
DECAY_SCALE = math.exp(-0.5)
NEG = -1e30
VMEM_LIMIT = 48 * 1024 * 1024


def _dot(a, b):
    return jnp.dot(a, b, preferred_element_type=F32)


def _dot_nt(a, b):
    return lax.dot_general(a, b, (((1,), (1,)), ((), ())), preferred_element_type=F32)


def _dot_tn(a, b):
    return lax.dot_general(a, b, (((0,), (0,)), ((), ())), preferred_element_type=F32)


def _split3(x):
    hi = x.astype(BF16)
    r1 = x - hi.astype(F32)
    mid = r1.astype(BF16)
    lo = (r1 - mid.astype(F32)).astype(BF16)
    return hi, mid, lo


def _sigmoid(x):
    return 1.0 / (1.0 + jnp.exp(-x))


def _norm_proj_kernel(h_ref, g_ref, w_ref, o_ref, t_ref, *, n_main, q_cols, q_scale):
    x = h_ref[...]
    var = jnp.mean(x * x, axis=-1, keepdims=True)
    u = (x * lax.rsqrt(var + NORM_EPS) * g_ref[...]).astype(BF16)
    for j in range(0, n_main, COL_TILE):
        acc = _dot(u, w_ref[:, j:j + COL_TILE])
        if j < q_cols:
            acc = acc * q_scale
        o_ref[:, j:j + COL_TILE] = acc.astype(BF16)
    t_ref[...] = _dot(u, w_ref[:, n_main:])


def _norm_proj(h, g, w, *, q_cols=0, q_scale=1.0):
    T, D = h.shape
    n_all = w.shape[1]
    n_main = n_all - LANES
    kern = functools.partial(_norm_proj_kernel, n_main=n_main, q_cols=q_cols, q_scale=q_scale)
    return pl.pallas_call(
        kern,
        grid=(T // ROW_TILE,),
        in_specs=[
            pl.BlockSpec((ROW_TILE, D), lambda i: (i, 0)),
            pl.BlockSpec((1, D), lambda i: (0, 0)),
            pl.BlockSpec((D, n_all), lambda i: (0, 0)),
        ],
        out_specs=[
            pl.BlockSpec((ROW_TILE, n_main), lambda i: (i, 0)),
            pl.BlockSpec((ROW_TILE, LANES), lambda i: (i, 0)),
        ],
        out_shape=[
            jax.ShapeDtypeStruct((T, n_main), BF16),
            jax.ShapeDtypeStruct((T, LANES), F32),
        ],
        compiler_params=pltpu.CompilerParams(
            dimension_semantics=("arbitrary",), vmem_limit_bytes=VMEM_LIMIT),
        name="norm_proj",
    )(h, g, w)


def _out_proj_kernel(z_ref, w_ref, h_ref, g_ref, o_ref):
    m = _dot(z_ref[...], w_ref[...])
    var = jnp.mean(m * m, axis=-1, keepdims=True)
    o_ref[...] = h_ref[...] + m * lax.rsqrt(var + NORM_EPS) * g_ref[...]


def _out_proj(z, w, h, g):
    T, D = h.shape
    return pl.pallas_call(
        _out_proj_kernel,
        grid=(T // ROW_TILE,),
        in_specs=[
            pl.BlockSpec((ROW_TILE, D), lambda i: (i, 0)),
            pl.BlockSpec((D, D), lambda i: (0, 0)),
            pl.BlockSpec((ROW_TILE, D), lambda i: (i, 0)),
            pl.BlockSpec((1, D), lambda i: (0, 0)),
        ],
        out_specs=pl.BlockSpec((ROW_TILE, D), lambda i: (i, 0)),
        out_shape=jax.ShapeDtypeStruct((T, D), F32),
        compiler_params=pltpu.CompilerParams(
            dimension_semantics=("arbitrary",), vmem_limit_bytes=VMEM_LIMIT),
        name="out_proj",
    )(z, w, h, g)


def _cum_kernel(x_ref, b_ref, o_ref, carry_ref):
    j = pl.program_id(0)

    @pl.when(j == 0)
    def _():
        carry_ref[...] = jnp.zeros_like(carry_ref)

    x = x_ref[...] + b_ref[...]
    lf = jnp.minimum(x, 0.0) - jnp.log(1.0 + jnp.exp(-jnp.abs(x)))
    row = lax.broadcasted_iota(jnp.int32, (BLK, BLK), 0)
    col = lax.broadcasted_iota(jnp.int32, (BLK, BLK), 1)
    tri = (row <= col).astype(BF16)
    hi, mid, lo = _split3(lf)
    out = _dot(hi, tri) + _dot(mid, tri) + _dot(lo, tri) + carry_ref[...]
    o_ref[...] = out
    carry_ref[...] = out[:, BLK - 1:BLK]


def _fox_cum(f_t, b_col):
    R, L = f_t.shape
    return pl.pallas_call(
        _cum_kernel,
        grid=(L // BLK,),
        in_specs=[
            pl.BlockSpec((R, BLK), lambda j: (0, j)),
            pl.BlockSpec((R, 1), lambda j: (0, 0)),
        ],
        out_specs=pl.BlockSpec((R, BLK), lambda j: (0, j)),
        out_shape=jax.ShapeDtypeStruct((R, L), F32),
        scratch_shapes=[pltpu.VMEM((R, 1), F32)],
        compiler_params=pltpu.CompilerParams(dimension_semantics=("arbitrary",)),
        name="fox_cum",
    )(f_t, b_col)


def _fox_attn_kernel(q_ref, k_ref, v_ref, g_ref, c_ref, o_ref):
    i = pl.program_id(2)
    lane = lax.broadcasted_iota(jnp.int32, (1, LANES), 1)
    is_a = lane < HEAD
    q2 = q_ref[0]
    zq = jnp.zeros_like(q2)
    qs = jnp.concatenate([jnp.where(is_a, q2, zq), jnp.where(is_a, zq, q2)], axis=0)
    cq = c_ref[0, 0, i]
    cref_a = cq[0:1, 0:1]
    cref_b = cq[1:2, 0:1]
    one = jnp.ones((), BF16)

    def step(j, carry, diag):
        m_a, m_b, acc_a, acc_b = carry
        off = pl.multiple_of(j * BLK, BLK)
        k2 = k_ref[0, pl.ds(off, BLK), :]
        v2 = v_ref[0, pl.ds(off, BLK), :]
        cj = c_ref[0, 0, j]
        s = _dot_nt(qs, k2)
        s_a = s[:BLK] + (cref_a - cj[0:1])
        s_b = s[BLK:] + (cref_b - cj[1:2])
        if diag:
            r = lax.broadcasted_iota(jnp.int32, (BLK, BLK), 0)
            c = lax.broadcasted_iota(jnp.int32, (BLK, BLK), 1)
            keep = c <= r
            s_a = jnp.where(keep, s_a, NEG)
            s_b = jnp.where(keep, s_b, NEG)
        mn_a = jnp.maximum(m_a, jnp.max(s_a, axis=1, keepdims=True))
        mn_b = jnp.maximum(m_b, jnp.max(s_b, axis=1, keepdims=True))
        p_a = jnp.exp(s_a - mn_a).astype(BF16)
        p_b = jnp.exp(s_b - mn_b).astype(BF16)
        v_a = jnp.where(is_a, v2, one)
        v_b = jnp.where(is_a, one, v2)
        acc_a = acc_a * jnp.exp(m_a - mn_a) + _dot(p_a, v_a)
        acc_b = acc_b * jnp.exp(m_b - mn_b) + _dot(p_b, v_b)
        return mn_a, mn_b, acc_a, acc_b

    init = (jnp.full((BLK, 1), NEG, F32), jnp.full((BLK, 1), NEG, F32),
            jnp.zeros((BLK, LANES), F32), jnp.zeros((BLK, LANES), F32))
    carry = lax.fori_loop(0, i, lambda j, c: step(j, c, False), init)
    _, _, acc_a, acc_b = step(i, carry, True)
    o_a = acc_a / pltpu.roll(acc_a, HEAD, axis=1)
    o_b = acc_b / pltpu.roll(acc_b, HEAD, axis=1)
    o = jnp.where(is_a, o_a, o_b)
    g = g_ref[0].astype(F32)
    o_ref[0] = (o * (g * _sigmoid(g))).astype(BF16)


def _fox_attn(p3, cum5):
    B, L, _ = p3.shape
    n_pair = (p3.shape[2] // 4) // LANES
    nq = L // BLK
    return pl.pallas_call(
        _fox_attn_kernel,
        grid=(B, n_pair, nq),
        in_specs=[
            pl.BlockSpec((1, BLK, LANES), lambda b, p, i: (b, i, p)),
            pl.BlockSpec((1, L, LANES), lambda b, p, i: (b, 0, n_pair + p)),
            pl.BlockSpec((1, L, LANES), lambda b, p, i: (b, 0, 2 * n_pair + p)),
            pl.BlockSpec((1, BLK, LANES), lambda b, p, i: (b, i, 3 * n_pair + p)),
            pl.BlockSpec((1, 1, nq, 2, BLK), lambda b, p, i: (b, p, 0, 0, 0)),
        ],
        out_specs=pl.BlockSpec((1, BLK, LANES), lambda b, p, i: (b, i, p)),
        out_shape=jax.ShapeDtypeStruct((B, L, n_pair * LANES), BF16),
        compiler_params=pltpu.CompilerParams(
            dimension_semantics=("arbitrary", "arbitrary", "arbitrary"), vmem_limit_bytes=VMEM_LIMIT),
        name="fox_attn",
    )(p3, p3, p3, p3, cum5)


def _head_sums(x):
    row = lax.broadcasted_iota(jnp.int32, (LANES, LANES), 0)
    col = lax.broadcasted_iota(jnp.int32, (LANES, LANES), 1)
    ones = ((row // HEAD) == (col // HEAD)).astype(BF16)
    parts = []
    for g in range(x.shape[1] // LANES):
        xs = x[:, g * LANES:(g + 1) * LANES]
        hi = xs.astype(BF16)
        lo = (xs - hi.astype(F32)).astype(BF16)
        parts.append(_dot(hi, ones) + _dot(lo, ones))
    return jnp.concatenate(parts, axis=1)


def _unit_lower_inverse(a):
    row = lax.broadcasted_iota(jnp.int32, (CHUNK, CHUNK), 0)
    col = lax.broadcasted_iota(jnp.int32, (CHUNK, CHUNK), 1)
    eye = (row == col).astype(F32)
    t = eye + jnp.where((row // 2 == col // 2) & (row > col), a, 0.0)
    s = 2
    while s < CHUNK:
        off = (row // (2 * s) == col // (2 * s)) & ((row // s) % 2 == 1) & ((col // s) % 2 == 0)
        a_off = jnp.where(off, a, 0.0).astype(BF16)
        tb = t.astype(BF16)
        t = t + _dot(_dot(tb, a_off).astype(BF16), tb)
        s *= 2
    return t


def _wkv_kernel(r_ref, k_ref, v_ref, g_ref, rh_ref, kh_ref, vh_ref, gh_ref, t_ref, th_ref,
                mu_ref, mut_ref, w0_ref, wup_ref, a0_ref, aup_ref, kk_ref, ka_ref, rk_ref,
                lnw_ref, lnb_ref, o_ref, st_ref):
    c = pl.program_id(1)

    @pl.when(c == 0)
    def _():
        st_ref[...] = jnp.zeros_like(st_ref)

    first = c == 0
    trow = lax.broadcasted_iota(jnp.int32, (CHUNK, 1), 0)

    def shifted(cur_ref, halo_ref, mu):
        cur = cur_ref[0].astype(F32)
        prev = jnp.where(first, 0.0, halo_ref[0, 7:8, :].astype(F32))
        sh = jnp.where(trow == 0, prev, pltpu.roll(cur, 1, axis=0))
        return cur + (sh - cur) * mu

    D = r_ref.shape[2]
    r = shifted(r_ref, rh_ref, mu_ref[:, 0:D])
    k = shifted(k_ref, kh_ref, mu_ref[:, D:2 * D])
    v = shifted(v_ref, vh_ref, mu_ref[:, 2 * D:3 * D])
    g = shifted(g_ref, gh_ref, mu_ref[:, 3 * D:4 * D])
    tl = shifted(t_ref, th_ref, mut_ref[...])
    wd = tl[:, :HEAD]
    ad = tl[:, HEAD:]

    w_log = w0_ref[...] + _dot(jnp.tanh(wd).astype(BF16), wup_ref[...])
    lw = -DECAY_SCALE * _sigmoid(w_log)
    a = _sigmoid(a0_ref[...] + _dot(ad.astype(BF16), aup_ref[...]))
    kk = k * kk_ref[...]
    kk = kk / jnp.maximum(jnp.sqrt(_head_sums(kk * kk)), 1e-12)
    k = k * (1.0 + (a - 1.0) * ka_ref[...])
    bv = kk * a

    row = lax.broadcasted_iota(jnp.int32, (CHUNK, CHUNK), 0)
    col = lax.broadcasted_iota(jnp.int32, (CHUNK, CHUNK), 1)
    tril = (col <= row).astype(BF16)
    hi, mid, lo = _split3(lw)
    cl = _dot(tril, hi) + _dot(tril, mid) + _dot(tril, lo)
    cl_end = cl[CHUNK - 1:CHUNK, :]
    e_neg = jnp.exp(-cl)
    e_end = jnp.exp(cl_end - cl)
    r_t = (r * jnp.exp(cl)).astype(BF16)
    a_t = (-kk * jnp.exp(cl - lw)).astype(BF16)
    b_t = (bv * e_neg).astype(BF16)
    k_t = (k * e_neg).astype(BF16)
    b_h = (bv * e_end).astype(BF16)
    k_h = (k * e_end).astype(BF16)
    w_end = jnp.exp(cl_end)
    v_b = v.astype(BF16)

    strict = col < row
    incl = col <= row
    ys = []
    for h in range(D // HEAD):
        sl = slice(h * HEAD, (h + 1) * HEAD)
        s0 = st_ref[h]
        s0b = s0.astype(BF16)
        x = jnp.concatenate([a_t[:, sl], r_t[:, sl]], axis=0)
        y = jnp.concatenate([k_t[:, sl], b_t[:, sl]], axis=0)
        sc = _dot_nt(x, y)
        a_ak = jnp.where(strict, sc[:CHUNK, :CHUNK], 0.0).astype(BF16)
        a_ab = jnp.where(strict, sc[:CHUNK, CHUNK:], 0.0)
        m_rk = jnp.where(incl, sc[CHUNK:, :CHUNK], 0.0).astype(BF16)
        m_rb = jnp.where(incl, sc[CHUNK:, CHUNK:], 0.0).astype(BF16)
        t_inv = _unit_lower_inverse(a_ab).astype(BF16)
        vh = v_b[:, sl]
        xs = _dot_nt(x, s0b)
        gmat = xs[:CHUNK] + _dot(a_ak, vh)
        u = _dot(t_inv, gmat.astype(BF16))
        ub = u.astype(BF16)
        ys.append(xs[CHUNK:] + _dot(m_rb, ub) + _dot(m_rk, vh))
        uv = jnp.concatenate([ub, vh], axis=0)
        bk = jnp.concatenate([b_h[:, sl], k_h[:, sl]], axis=0)
        st_ref[h] = s0 * w_end[:, sl] + _dot_tn(uv, bk)
    y = jnp.concatenate(ys, axis=1)

    inv_n = 1.0 / HEAD
    mean = _head_sums(y) * inv_n
    d = y - mean
    var = _head_sums(d * d) * inv_n
    yn = d * lax.rsqrt(var + GN_EPS) * lnw_ref[...] + lnb_ref[...]
    bonus = _head_sums(r * k * rk_ref[...]) * v
    o_ref[0] = ((yn + bonus) * (g * _sigmoid(g))).astype(BF16)


def _wkv(p3, tail3, mu, mu_t, w0, w_up, a0, a_up, k_k, k_a, r_k, ln_w, ln_b):
    B, L, n_main = p3.shape
    D = n_main // 4
    nc = L // CHUNK
    sub = CHUNK // 8

    def cur(s):
        return pl.BlockSpec((1, CHUNK, D), lambda b, c: (b, c, s))

    def halo(s):
        return pl.BlockSpec((1, 8, D), lambda b, c: (b, jnp.maximum(c * sub - 1, 0), s))

    def vec(n):
        return pl.BlockSpec((1, n), lambda b, c: (0, 0))

    lora = pl.BlockSpec((HEAD, D), lambda b, c: (0, 0))
    return pl.pallas_call(
        _wkv_kernel,
        grid=(B, nc),
        in_specs=[cur(0), cur(1), cur(2), cur(3), halo(0), halo(1), halo(2), halo(3),
                  pl.BlockSpec((1, CHUNK, LANES), lambda b, c: (b, c, 0)),
                  pl.BlockSpec((1, 8, LANES), lambda b, c: (b, jnp.maximum(c * sub - 1, 0), 0)),
                  vec(n_main), vec(LANES), vec(D), lora, vec(D), lora,
                  vec(D), vec(D), vec(D), vec(D), vec(D)],
        out_specs=pl.BlockSpec((1, CHUNK, D), lambda b, c: (b, c, 0)),
        out_shape=jax.ShapeDtypeStruct((B, L, D), BF16),
        scratch_shapes=[pltpu.VMEM((D // HEAD, HEAD, HEAD), F32)],
        compiler_params=pltpu.CompilerParams(
            dimension_semantics=("arbitrary", "arbitrary"), vmem_limit_bytes=VMEM_LIMIT),
        name="wkv7",
    )(p3, p3, p3, p3, p3, p3, p3, p3, tail3, tail3,
      mu, mu_t, w0, w_up, a0, a_up, k_k, k_a, r_k, ln_w, ln_b)


def _fox_layer(h, B, L, g_pre, g_post, w_in, b_f, w_out):
    D = h.shape[1]
    n_heads = b_f.shape[0]
    n_main = 4 * D
    w_cat = jnp.concatenate(
        [w_in, jnp.zeros((D, LANES - n_heads), w_in.dtype)], axis=1).astype(BF16)
    p, tail = _norm_proj(h, g_pre.reshape(1, D), w_cat, q_cols=D, q_scale=HEAD ** -0.5)
    f_t = tail[:, :n_heads].reshape(B, L, n_heads).transpose(0, 2, 1).reshape(B * n_heads, L)
    b_col = jnp.tile(b_f, B).reshape(B * n_heads, 1)
    cum = _fox_cum(f_t, b_col)
    cum5 = cum.reshape(B, n_heads // 2, 2, L // BLK, BLK).transpose(0, 1, 3, 2, 4)
    z = _fox_attn(p.reshape(B, L, n_main), cum5)
    return _out_proj(z.reshape(B * L, D), w_out.astype(BF16), h, g_post.reshape(1, D))


def _rwkv_layer(h, B, L, g_pre, g_post, w_in, mu, w0, w_up, a0, a_up, k_k, k_a, r_k,
                ln_w, ln_b, w_out):
    D = h.shape[1]
    n_main = 4 * D
    p, tail = _norm_proj(h, g_pre.reshape(1, D), w_in.astype(BF16))
    row = lambda t: t.reshape(1, -1)
    z = _wkv(p.reshape(B, L, n_main), tail.reshape(B, L, LANES),
             row(mu[:n_main]), row(mu[n_main:]), row(w0), w_up.astype(BF16), row(a0),
             a_up.astype(BF16), row(k_k), row(k_a), row(r_k), row(ln_w), row(ln_b))
    return _out_proj(z.reshape(B * L, D), w_out.astype(BF16), h, g_post.reshape(1, D))


def kernel(x, meta_tokens, norm_pre, norm_post, fox_w_in, fox_b_f, fox_w_out, rwkv_w_in, rwkv_mu, rwkv_w0, rwkv_w_up, rwkv_a0, rwkv_a_up, rwkv_k_k, rwkv_k_a, rwkv_r_k, rwkv_ln_w, rwkv_ln_b, rwkv_w_out):
    B, S, D = x.shape
    depth = norm_pre.shape[0]
    L0 = N_META + S
    L = -(-L0 // BLK) * BLK
    assert (B * L) % ROW_TILE == 0 and D % LANES == 0
    meta = jnp.broadcast_to(meta_tokens[None].astype(x.dtype), (B, N_META, D))
    h = jnp.concatenate([meta, x, jnp.zeros((B, L - L0, D), x.dtype)], axis=1).reshape(B * L, D)
    for i in range(depth):
        j = i // 2
        if i % 2 == 0:
            h = _fox_layer(h, B, L, norm_pre[i], norm_post[i], fox_w_in[j], fox_b_f[j], fox_w_out[j])
        else:
            h = _rwkv_layer(h, B, L, norm_pre[i], norm_post[i], rwkv_w_in[j], rwkv_mu[j],
                            rwkv_w0[j], rwkv_w_up[j], rwkv_a0[j], rwkv_a_up[j], rwkv_k_k[j],
                            rwkv_k_a[j], rwkv_r_k[j], rwkv_ln_w[j], rwkv_ln_b[j], rwkv_w_out[j])
    return h.reshape(B, L, D)[:, N_META:L0]
```

```python
import functools
import math

import jax
import jax.numpy as jnp
from jax import lax
from jax.experimental import pallas as pl
from jax.experimental.pallas import tpu as pltpu

F32 = jnp.float32
BF16 = jnp.bfloat16

N_META = 16
HEAD = 64
LANES = 128
BLK = 128
ATTN_PAIRS = 4
CHUNK = 64
ROW_TILE = 512
COL_TILE = 512
NORM_EPS = 1e-6
GN_EPS = 64e-5
DECAY_SCALE = math.exp(-0.5)
NEG = -1e30
VMEM_LIMIT = 48 * 1024 * 1024


def _dot(a, b):
    return jnp.dot(a, b, preferred_element_type=F32)


def _dot_nt(a, b):
    return lax.dot_general(a, b, (((1,), (1,)), ((), ())), preferred_element_type=F32)


def _dot_tn(a, b):
    return lax.dot_general(a, b, (((0,), (0,)), ((), ())), preferred_element_type=F32)


def _split3(x):
    hi = x.astype(BF16)
    r1 = x - hi.astype(F32)
    mid = r1.astype(BF16)
    lo = (r1 - mid.astype(F32)).astype(BF16)
    return hi, mid, lo


def _sigmoid(x):
    return 1.0 / (1.0 + jnp.exp(-x))


def _norm_proj_kernel(h_ref, g_ref, w_ref, o_ref, t_ref, *, n_main, q_cols, q_scale):
    x = h_ref[...]
    var = jnp.mean(x * x, axis=-1, keepdims=True)
    u = (x * lax.rsqrt(var + NORM_EPS) * g_ref[...]).astype(BF16)
    for j in range(0, n_main, COL_TILE):
        acc = _dot(u, w_ref[:, j:j + COL_TILE])
        if j < q_cols:
            acc = acc * q_scale
        o_ref[:, j:j + COL_TILE] = acc.astype(BF16)
    t_ref[...] = _dot(u, w_ref[:, n_main:])


def _norm_proj(h, g, w, *, q_cols=0, q_scale=1.0):
    T, D = h.shape
    n_all = w.shape[1]
    n_main = n_all - LANES
    kern = functools.partial(_norm_proj_kernel, n_main=n_main, q_cols=q_cols, q_scale=q_scale)
    return pl.pallas_call(
        kern,
        grid=(T // ROW_TILE,),
        in_specs=[
            pl.BlockSpec((ROW_TILE, D), lambda i: (i, 0)),
            pl.BlockSpec((1, D), lambda i: (0, 0)),
            pl.BlockSpec((D, n_all), lambda i: (0, 0)),
        ],
        out_specs=[
            pl.BlockSpec((ROW_TILE, n_main), lambda i: (i, 0)),
            pl.BlockSpec((ROW_TILE, LANES), lambda i: (i, 0)),
        ],
        out_shape=[
            jax.ShapeDtypeStruct((T, n_main), BF16),
            jax.ShapeDtypeStruct((T, LANES), F32),
        ],
        compiler_params=pltpu.CompilerParams(
            dimension_semantics=("arbitrary",), vmem_limit_bytes=VMEM_LIMIT),
        name="norm_proj",
    )(h, g, w)


def _out_proj_kernel(z_ref, w_ref, h_ref, g_ref, o_ref):
    m = _dot(z_ref[...], w_ref[...])
    var = jnp.mean(m * m, axis=-1, keepdims=True)
    o_ref[...] = h_ref[...] + m * lax.rsqrt(var + NORM_EPS) * g_ref[...]


def _out_proj(z, w, h, g):
    T, D = h.shape
    return pl.pallas_call(
        _out_proj_kernel,
        grid=(T // ROW_TILE,),
        in_specs=[
            pl.BlockSpec((ROW_TILE, D), lambda i: (i, 0)),
            pl.BlockSpec((D, D), lambda i: (0, 0)),
            pl.BlockSpec((ROW_TILE, D), lambda i: (i, 0)),
            pl.BlockSpec((1, D), lambda i: (0, 0)),
        ],
        out_specs=pl.BlockSpec((ROW_TILE, D), lambda i: (i, 0)),
        out_shape=jax.ShapeDtypeStruct((T, D), F32),
        compiler_params=pltpu.CompilerParams(
            dimension_semantics=("arbitrary",), vmem_limit_bytes=VMEM_LIMIT),
        name="out_proj",
    )(z, w, h, g)


def _cum_kernel(x_ref, b_ref, o_ref, carry_ref):
    j = pl.program_id(0)

    @pl.when(j == 0)
    def _():
        carry_ref[...] = jnp.zeros_like(carry_ref)

    x = x_ref[...] + b_ref[...]
    lf = jnp.minimum(x, 0.0) - jnp.log(1.0 + jnp.exp(-jnp.abs(x)))
    row = lax.broadcasted_iota(jnp.int32, (BLK, BLK), 0)
    col = lax.broadcasted_iota(jnp.int32, (BLK, BLK), 1)
    tri = (row <= col).astype(BF16)
    hi, mid, lo = _split3(lf)
    out = _dot(hi, tri) + _dot(mid, tri) + _dot(lo, tri) + carry_ref[...]
    o_ref[...] = out
    carry_ref[...] = out[:, BLK - 1:BLK]


def _fox_cum(f_t, b_col):
    R, L = f_t.shape
    return pl.pallas_call(
        _cum_kernel,
        grid=(L // BLK,),
        in_specs=[
            pl.BlockSpec((R, BLK), lambda j: (0, j)),
            pl.BlockSpec((R, 1), lambda j: (0, 0)),
        ],
        out_specs=pl.BlockSpec((R, BLK), lambda j: (0, j)),
        out_shape=jax.ShapeDtypeStruct((R, L), F32),
        scratch_shapes=[pltpu.VMEM((R, 1), F32)],
        compiler_params=pltpu.CompilerParams(dimension_semantics=("arbitrary",)),
        name="fox_cum",
    )(f_t, b_col)


def _fox_attn_kernel(q_ref, k_ref, v_ref, g_ref, c_ref, o_ref):
    i = pl.program_id(2)
    n_p = q_ref.shape[2] // LANES
    pairs = range(n_p)
    ps = [slice(p * LANES, (p + 1) * LANES) for p in pairs]
    lane = lax.broadcasted_iota(jnp.int32, (1, LANES), 1)
    is_a = lane < HEAD
    zq = jnp.zeros((), BF16)
    one = jnp.ones((), BF16)
    qs = [jnp.concatenate([jnp.where(is_a, q_ref[0, :, s], zq), jnp.where(is_a, zq, q_ref[0, :, s])],
                          axis=0) for s in ps]
    cref = [c_ref[0, p, i][:, 0:1] for p in pairs]

    def step(j, carry, diag):
        off = pl.multiple_of(j * BLK, BLK)
        sc = [_dot_nt(qs[p], k_ref[0, pl.ds(off, BLK), ps[p]]) for p in pairs]
        if diag:
            r = lax.broadcasted_iota(jnp.int32, (BLK, BLK), 0)
            c = lax.broadcasted_iota(jnp.int32, (BLK, BLK), 1)
            keep = c <= r
        s_h, mn = [], []
        for p in pairs:
            bias = cref[p] - c_ref[0, p, j]
            for h in range(2):
                s = sc[p][h * BLK:(h + 1) * BLK] + bias[h:h + 1]
                if diag:
                    s = jnp.where(keep, s, NEG)
                s_h.append(s)
                mn.append(jnp.maximum(carry[0][2 * p + h], jnp.max(s, axis=1, keepdims=True)))
        pr = [jnp.exp(s_h[n] - mn[n]).astype(BF16) for n in range(2 * n_p)]
        acc = []
        for p in pairs:
            v2 = v_ref[0, pl.ds(off, BLK), ps[p]]
            for h in range(2):
                n = 2 * p + h
                vh = jnp.where(is_a, v2, one) if h == 0 else jnp.where(is_a, one, v2)
                acc.append(carry[1][n] * jnp.exp(carry[0][n] - mn[n]) + _dot(pr[n], vh))
        return tuple(mn), tuple(acc)

    init = (tuple(jnp.full((BLK, 1), NEG, F32) for _ in range(2 * n_p)),
            tuple(jnp.zeros((BLK, LANES), F32) for _ in range(2 * n_p)))
    carry = lax.fori_loop(0, i, lambda j, c: step(j, c, False), init)
    _, acc = step(i, carry, True)
    for p in pairs:
        o_a = acc[2 * p] / pltpu.roll(acc[2 * p], HEAD, axis=1)
        o_b = acc[2 * p + 1] / pltpu.roll(acc[2 * p + 1], HEAD, axis=1)
        g = g_ref[0, :, ps[p]].astype(F32)
        o_ref[0, :, ps[p]] = (jnp.where(is_a, o_a, o_b) * (g * _sigmoid(g))).astype(BF16)


def _fox_attn(p3, cum5):
    B, L, _ = p3.shape
    n_pair = (p3.shape[2] // 4) // LANES
    nq = L // BLK
    n_p = ATTN_PAIRS
    width = n_p * LANES
    n_grp = n_pair // n_p
    return pl.pallas_call(
        _fox_attn_kernel,
        grid=(B, n_grp, nq),
        in_specs=[
            pl.BlockSpec((1, BLK, width), lambda b, p, i: (b, i, p)),
            pl.BlockSpec((1, L, width), lambda b, p, i: (b, 0, n_grp + p)),
            pl.BlockSpec((1, L, width), lambda b, p, i: (b, 0, 2 * n_grp + p)),
            pl.BlockSpec((1, BLK, width), lambda b, p, i: (b, i, 3 * n_grp + p)),
            pl.BlockSpec((1, n_p, nq, 2, BLK), lambda b, p, i: (b, p, 0, 0, 0)),
        ],
        out_specs=pl.BlockSpec((1, BLK, width), lambda b, p, i: (b, i, p)),
        out_shape=jax.ShapeDtypeStruct((B, L, n_pair * LANES), BF16),
        compiler_params=pltpu.CompilerParams(
            dimension_semantics=("arbitrary", "arbitrary", "arbitrary"), vmem_limit_bytes=VMEM_LIMIT),
        name="fox_attn",
    )(p3, p3, p3, p3, cum5)


def _head_sums(x):
    row = lax.broadcasted_iota(jnp.int32, (LANES, LANES), 0)
    col = lax.broadcasted_iota(jnp.int32, (LANES, LANES), 1)
    ones = ((row // HEAD) == (col // HEAD)).astype(BF16)
    parts = []
    for g in range(x.shape[1] // LANES):
        parts.append(_dot(x[:, g * LANES:(g + 1) * LANES].astype(BF16), ones))
    return jnp.concatenate(parts, axis=1)


def _wkv_kernel(r_ref, k_ref, v_ref, g_ref, rh_ref, kh_ref, vh_ref, gh_ref, t_ref, th_ref,
                mu_ref, mut_ref, w0_ref, wup_ref, a0_ref, aup_ref, kk_ref, ka_ref, rk_ref,
                lnw_ref, lnb_ref, o_ref, st_ref):
    c = pl.program_id(1)

    @pl.when(c == 0)
    def _():
        st_ref[...] = jnp.zeros_like(st_ref)

    first = c == 0
    trow = lax.broadcasted_iota(jnp.int32, (CHUNK, 1), 0)

    def shifted(cur_ref, halo_ref, mu):
        cur = cur_ref[0].astype(F32)
        prev = jnp.where(first, 0.0, halo_ref[0, 7:8, :].astype(F32))
        sh = jnp.where(trow == 0, prev, pltpu.roll(cur, 1, axis=0))
        return cur + (sh - cur) * mu

    D = r_ref.shape[2]
    r = shifted(r_ref, rh_ref, mu_ref[:, 0:D])
    k = shifted(k_ref, kh_ref, mu_ref[:, D:2 * D])
    v = shifted(v_ref, vh_ref, mu_ref[:, 2 * D:3 * D])
    g = shifted(g_ref, gh_ref, mu_ref[:, 3 * D:4 * D])
    tl = shifted(t_ref, th_ref, mut_ref[...])
    wd = tl[:, :HEAD]
    ad = tl[:, HEAD:]

    w_log = w0_ref[...] + _dot(jnp.tanh(wd).astype(BF16), wup_ref[...])
    lw = -DECAY_SCALE * _sigmoid(w_log)
    a = _sigmoid(a0_ref[...] + _dot(ad.astype(BF16), aup_ref[...]))
    kk = k * kk_ref[...]
    kk = kk / jnp.maximum(jnp.sqrt(_head_sums(kk * kk)), 1e-12)
    k = k * (1.0 + (a - 1.0) * ka_ref[...])
    bv = kk * a

    row = lax.broadcasted_iota(jnp.int32, (CHUNK, CHUNK), 0)
    col = lax.broadcasted_iota(jnp.int32, (CHUNK, CHUNK), 1)
    tril = (col <= row).astype(BF16)
    hi, mid, lo = _split3(lw)
    cl = _dot(tril, hi) + _dot(tril, mid) + _dot(tril, lo)
    cl_end = cl[CHUNK - 1:CHUNK, :]
    e_neg = jnp.exp(-cl)
    e_end = jnp.exp(cl_end - cl)
    r_t = (r * jnp.exp(cl)).astype(BF16)
    a_t = (-kk * jnp.exp(cl - lw)).astype(BF16)
    b_t = (bv * e_neg).astype(BF16)
    k_t = (k * e_neg).astype(BF16)
    b_h = (bv * e_end).astype(BF16)
    k_h = (k * e_end).astype(BF16)
    w_end = jnp.exp(cl_end)
    v_b = v.astype(BF16)

    n_pair = D // LANES
    lane = lax.broadcasted_iota(jnp.int32, (1, LANES), 1)
    is_a = lane < HEAD
    trw = lax.broadcasted_iota(jnp.int32, (CHUNK, LANES), 0)
    jw = lax.broadcasted_iota(jnp.int32, (CHUNK, LANES), 1) % HEAD
    strict = jw < trw
    incl = jw <= trw
    r2 = lax.broadcasted_iota(jnp.int32, (LANES, LANES), 0) // HEAD
    c2 = lax.broadcasted_iota(jnp.int32, (LANES, LANES), 1) // HEAD
    same_head = r2 == c2

    def bdiag(xw, swap=False):
        za = jnp.where(is_a, xw, jnp.zeros_like(xw))
        zb = jnp.where(is_a, jnp.zeros_like(xw), xw)
        return jnp.concatenate([zb, za] if swap else [za, zb], axis=0)

    pairs = range(n_pair)
    ps = [slice(p * LANES, (p + 1) * LANES) for p in pairs]
    xq = [jnp.concatenate([a_t[:, s], r_t[:, s]], axis=0) for s in ps]
    ycat = [jnp.concatenate(
        [jnp.where(is_a, jnp.concatenate([k_t[:, s], b_t[:, s]], axis=0), jnp.zeros((), BF16)),
         jnp.where(is_a, jnp.zeros((), BF16), jnp.concatenate([b_t[:, s], k_t[:, s]], axis=0))],
        axis=0) for s in ps]
    sc = [_dot_nt(xq[p], ycat[p]) for p in pairs]
    s0 = [st_ref[p] for p in pairs]
    xs = [_dot_nt(xq[p], s0[p].astype(BF16)) for p in pairs]
    top_lo = [sc[p][:CHUNK, :LANES] for p in pairs]
    top_hi = [sc[p][:CHUNK, LANES:] for p in pairs]
    bot_lo = [sc[p][CHUNK:, :LANES] for p in pairs]
    bot_hi = [sc[p][CHUNK:, LANES:] for p in pairs]
    a_ak = [jnp.where(strict, jnp.where(is_a, top_lo[p], top_hi[p]), 0.0).astype(BF16) for p in pairs]
    a_ab = [jnp.where(strict, jnp.where(is_a, top_hi[p], top_lo[p]), 0.0) for p in pairs]
    m_rk = [jnp.where(incl, jnp.where(is_a, bot_lo[p], bot_hi[p]), 0.0).astype(BF16) for p in pairs]
    m_rb = [jnp.where(incl, jnp.where(is_a, bot_hi[p], bot_lo[p]), 0.0).astype(BF16) for p in pairs]

    eye = (jw == trw).astype(F32)
    lvl1 = (trw // 2 == jw // 2) & strict
    tw = [eye + jnp.where(lvl1, a_ab[p], 0.0) for p in pairs]
    s = 2
    while s < CHUNK:
        off = (trw // (2 * s) == jw // (2 * s)) & ((trw // s) % 2 == 1) & ((jw // s) % 2 == 0)
        tb = [t.astype(BF16) for t in tw]
        pm = [_dot(tb[p], bdiag(jnp.where(off, a_ab[p], 0.0).astype(BF16))) for p in pairs]
        tw = [tw[p] + _dot(pm[p].astype(BF16), bdiag(tb[p])) for p in pairs]
        s *= 2

    vbd = [bdiag(v_b[:, s]) for s in ps]
    gm = [xs[p][:CHUNK] + _dot(a_ak[p], vbd[p]) for p in pairs]
    u = [_dot(tw[p].astype(BF16), bdiag(gm[p].astype(BF16), swap=True)).astype(BF16) for p in pairs]
    ys = [xs[p][CHUNK:] + _dot(jnp.concatenate([m_rb[p], m_rk[p]], axis=1),
                               jnp.concatenate([bdiag(u[p], swap=True), vbd[p]], axis=0))
          for p in pairs]
    for p in pairs:
        uv = jnp.concatenate([u[p], v_b[:, ps[p]]], axis=0)
        bk = jnp.concatenate([b_h[:, ps[p]], k_h[:, ps[p]]], axis=0)
        st_ref[p] = s0[p] * w_end[:, ps[p]] + jnp.where(same_head, _dot_tn(uv, bk), 0.0)
    y = jnp.concatenate(ys, axis=1)

    inv_n = 1.0 / HEAD
    mean = _head_sums(y) * inv_n
    d = y - mean
    var = _head_sums(d * d) * inv_n
    yn = d * lax.rsqrt(var + GN_EPS) * lnw_ref[...] + lnb_ref[...]
    bonus = _head_sums(r * k * rk_ref[...]) * v
    o_ref[0] = ((yn + bonus) * (g * _sigmoid(g))).astype(BF16)


def _wkv(p3, tail3, mu, mu_t, w0, w_up, a0, a_up, k_k, k_a, r_k, ln_w, ln_b):
    B, L, n_main = p3.shape
    D = n_main // 4
    nc = L // CHUNK
    sub = CHUNK // 8

    def cur(s):
        return pl.BlockSpec((1, CHUNK, D), lambda b, c: (b, c, s))

    def halo(s):
        return pl.BlockSpec((1, 8, D), lambda b, c: (b, jnp.maximum(c * sub - 1, 0), s))

    def vec(n):
        return pl.BlockSpec((1, n), lambda b, c: (0, 0))

    lora = pl.BlockSpec((HEAD, D), lambda b, c: (0, 0))
    return pl.pallas_call(
        _wkv_kernel,
        grid=(B, nc),
        in_specs=[cur(0), cur(1), cur(2), cur(3), halo(0), halo(1), halo(2), halo(3),
                  pl.BlockSpec((1, CHUNK, LANES), lambda b, c: (b, c, 0)),
                  pl.BlockSpec((1, 8, LANES), lambda b, c: (b, jnp.maximum(c * sub - 1, 0), 0)),
                  vec(n_main), vec(LANES), vec(D), lora, vec(D), lora,
                  vec(D), vec(D), vec(D), vec(D), vec(D)],
        out_specs=pl.BlockSpec((1, CHUNK, D), lambda b, c: (b, c, 0)),
        out_shape=jax.ShapeDtypeStruct((B, L, D), BF16),
        scratch_shapes=[pltpu.VMEM((D // LANES, LANES, LANES), F32)],
        compiler_params=pltpu.CompilerParams(
            dimension_semantics=("arbitrary", "arbitrary"), vmem_limit_bytes=VMEM_LIMIT),
        name="wkv7",
    )(p3, p3, p3, p3, p3, p3, p3, p3, tail3, tail3,
      mu, mu_t, w0, w_up, a0, a_up, k_k, k_a, r_k, ln_w, ln_b)


def _fox_layer(h, B, L, g_pre, g_post, w_in, b_f, w_out):
    D = h.shape[1]
    n_heads = b_f.shape[0]
    n_main = 4 * D
    w_cat = jnp.concatenate(
        [w_in, jnp.zeros((D, LANES - n_heads), w_in.dtype)], axis=1).astype(BF16)
    p, tail = _norm_proj(h, g_pre.reshape(1, D), w_cat, q_cols=D, q_scale=HEAD ** -0.5)
    f_t = tail[:, :n_heads].reshape(B, L, n_heads).transpose(0, 2, 1).reshape(B * n_heads, L)
    b_col = jnp.tile(b_f, B).reshape(B * n_heads, 1)
    cum = _fox_cum(f_t, b_col)
    cum5 = cum.reshape(B, n_heads // 2, 2, L // BLK, BLK).transpose(0, 1, 3, 2, 4)
    z = _fox_attn(p.reshape(B, L, n_main), cum5)
    return _out_proj(z.reshape(B * L, D), w_out.astype(BF16), h, g_post.reshape(1, D))


def _rwkv_layer(h, B, L, g_pre, g_post, w_in, mu, w0, w_up, a0, a_up, k_k, k_a, r_k,
                ln_w, ln_b, w_out):
    D = h.shape[1]
    n_main = 4 * D
    p, tail = _norm_proj(h, g_pre.reshape(1, D), w_in.astype(BF16))
    row = lambda t: t.reshape(1, -1)
    z = _wkv(p.reshape(B, L, n_main), tail.reshape(B, L, LANES),
             row(mu[:n_main]), row(mu[n_main:]), row(w0), w_up.astype(BF16), row(a0),
             a_up.astype(BF16), row(k_k), row(k_a), row(r_k), row(ln_w), row(ln_b))
    return _out_proj(z.reshape(B * L, D), w_out.astype(BF16), h, g_post.reshape(1, D))


def kernel(x, meta_tokens, norm_pre, norm_post, fox_w_in, fox_b_f, fox_w_out, rwkv_w_in, rwkv_mu, rwkv_w0, rwkv_w_up, rwkv_a0, rwkv_a_up, rwkv_k_k, rwkv_k_a, rwkv_r_k, rwkv_ln_w, rwkv_ln_b, rwkv_w_out):
    B, S, D = x.shape
    depth = norm_pre.shape[0]
    L0 = N_META + S
    L = -(-L0 // BLK) * BLK
    assert (B * L) % ROW_TILE == 0 and D % LANES == 0
    meta = jnp.broadcast_to(meta_tokens[None].astype(x.dtype), (B, N_META, D))
    h = jnp.concatenate([meta, x, jnp.zeros((B, L - L0, D), x.dtype)], axis=1).reshape(B * L, D)
    for i in range(depth):
        j = i // 2
        if i % 2 == 0:
            h = _fox_layer(h, B, L, norm_pre[i], norm_post[i], fox_w_in[j], fox_b_f[j], fox_w_out[j])
        else:
            h = _rwkv_layer(h, B, L, norm_pre[i], norm_post[i], rwkv_w_in[j], rwkv_mu[j],
                            rwkv_w0[j], rwkv_w_up[j], rwkv_a0[j], rwkv_a_up[j], rwkv_k_k[j],
                            rwkv_k_a[j], rwkv_r_k[j], rwkv_ln_w[j], rwkv_ln_b[j], rwkv_w_out[j])
    return h.reshape(B, L, D)[:, N_META:L0]
```

```python
import functools
import math

import jax
import jax.numpy as jnp
from jax import lax
from jax.experimental import pallas as pl
from jax.experimental.pallas import tpu as pltpu

F32 = jnp.float32
BF16 = jnp.bfloat16

N_META = 16
HEAD = 64
LANES = 128
BLK = 128
ATTN_PAIRS = 8
CHUNK = 64
WKV_BATCH = 2
ROW_TILE = 512
COL_TILE = 512
NORM_EPS = 1e-6
GN_EPS = 64e-5
DECAY_SCALE = math.exp(-0.5)
NEG = -1e30
VMEM_LIMIT = 48 * 1024 * 1024


def _dot(a, b):
    return jnp.dot(a, b, preferred_element_type=F32)


def _dot_nt(a, b):
    return lax.dot_general(a, b, (((1,), (1,)), ((), ())), preferred_element_type=F32)


def _dot_tn(a, b):
    return lax.dot_general(a, b, (((0,), (0,)), ((), ())), preferred_element_type=F32)


def _split3(x):
    hi = x.astype(BF16)
    r1 = x - hi.astype(F32)
    mid = r1.astype(BF16)
    lo = (r1 - mid.astype(F32)).astype(BF16)
    return hi, mid, lo


def _sigmoid(x):
    return 0.5 * jnp.tanh(0.5 * x) + 0.5


def _norm_proj_kernel(h_ref, g_ref, w_ref, o_ref, t_ref, *, n_main, q_cols, q_scale):
    x = h_ref[...]
    var = jnp.mean(x * x, axis=-1, keepdims=True)
    u = (x * lax.rsqrt(var + NORM_EPS) * g_ref[...]).astype(BF16)
    for j in range(0, n_main, COL_TILE):
        acc = _dot(u, w_ref[:, j:j + COL_TILE])
        if j < q_cols:
            acc = acc * q_scale
        o_ref[:, j:j + COL_TILE] = acc.astype(BF16)
    t_ref[...] = _dot(u, w_ref[:, n_main:])


def _norm_proj(h, g, w, *, q_cols=0, q_scale=1.0):
    T, D = h.shape
    n_all = w.shape[1]
    n_main = n_all - LANES
    kern = functools.partial(_norm_proj_kernel, n_main=n_main, q_cols=q_cols, q_scale=q_scale)
    return pl.pallas_call(
        kern,
        grid=(T // ROW_TILE,),
        in_specs=[
            pl.BlockSpec((ROW_TILE, D), lambda i: (i, 0)),
            pl.BlockSpec((1, D), lambda i: (0, 0)),
            pl.BlockSpec((D, n_all), lambda i: (0, 0)),
        ],
        out_specs=[
            pl.BlockSpec((ROW_TILE, n_main), lambda i: (i, 0)),
            pl.BlockSpec((ROW_TILE, LANES), lambda i: (i, 0)),
        ],
        out_shape=[
            jax.ShapeDtypeStruct((T, n_main), BF16),
            jax.ShapeDtypeStruct((T, LANES), F32),
        ],
        compiler_params=pltpu.CompilerParams(
            dimension_semantics=("arbitrary",), vmem_limit_bytes=VMEM_LIMIT),
        name="norm_proj",
    )(h, g, w)


def _out_proj_kernel(z_ref, w_ref, h_ref, g_ref, o_ref):
    m = _dot(z_ref[...], w_ref[...])
    var = jnp.mean(m * m, axis=-1, keepdims=True)
    o_ref[...] = h_ref[...] + m * lax.rsqrt(var + NORM_EPS) * g_ref[...]


def _out_proj(z, w, h, g):
    T, D = h.shape
    return pl.pallas_call(
        _out_proj_kernel,
        grid=(T // ROW_TILE,),
        in_specs=[
            pl.BlockSpec((ROW_TILE, D), lambda i: (i, 0)),
            pl.BlockSpec((D, D), lambda i: (0, 0)),
            pl.BlockSpec((ROW_TILE, D), lambda i: (i, 0)),
            pl.BlockSpec((1, D), lambda i: (0, 0)),
        ],
        out_specs=pl.BlockSpec((ROW_TILE, D), lambda i: (i, 0)),
        out_shape=jax.ShapeDtypeStruct((T, D), F32),
        compiler_params=pltpu.CompilerParams(
            dimension_semantics=("arbitrary",), vmem_limit_bytes=VMEM_LIMIT),
        name="out_proj",
    )(z, w, h, g)


def _cum_kernel(x_ref, b_ref, o_ref, carry_ref):
    j = pl.program_id(0)

    @pl.when(j == 0)
    def _():
        carry_ref[...] = jnp.zeros_like(carry_ref)

    x = x_ref[...] + b_ref[...]
    lf = jnp.minimum(x, 0.0) - jnp.log(1.0 + jnp.exp(-jnp.abs(x)))
    row = lax.broadcasted_iota(jnp.int32, (BLK, BLK), 0)
    col = lax.broadcasted_iota(jnp.int32, (BLK, BLK), 1)
    tri = (row <= col).astype(BF16)
    hi, mid, lo = _split3(lf)
    out = _dot(hi, tri) + _dot(mid, tri) + _dot(lo, tri) + carry_ref[...]
    o_ref[...] = out
    carry_ref[...] = out[:, BLK - 1:BLK]


def _fox_cum(f_t, b_col):
    R, L = f_t.shape
    return pl.pallas_call(
        _cum_kernel,
        grid=(L // BLK,),
        in_specs=[
            pl.BlockSpec((R, BLK), lambda j: (0, j)),
            pl.BlockSpec((R, 1), lambda j: (0, 0)),
        ],
        out_specs=pl.BlockSpec((R, BLK), lambda j: (0, j)),
        out_shape=jax.ShapeDtypeStruct((R, L), F32),
        scratch_shapes=[pltpu.VMEM((R, 1), F32)],
        compiler_params=pltpu.CompilerParams(dimension_semantics=("arbitrary",)),
        name="fox_cum",
    )(f_t, b_col)


def _fox_attn_kernel(q_ref, k_ref, v_ref, g_ref, c_ref, o_ref):
    i = pl.program_id(2)
    n_p = q_ref.shape[2] // LANES
    pairs = range(n_p)
    ps = [slice(p * LANES, (p + 1) * LANES) for p in pairs]
    lane = lax.broadcasted_iota(jnp.int32, (1, LANES), 1)
    is_a = lane < HEAD
    zq = jnp.zeros((), BF16)
    one = jnp.ones((), BF16)
    qs = [jnp.concatenate([jnp.where(is_a, q_ref[0, :, s], zq), jnp.where(is_a, zq, q_ref[0, :, s])],
                          axis=0) for s in ps]
    cref = [c_ref[0, p, i][:, 0:1] for p in pairs]

    r_i = lax.broadcasted_iota(jnp.int32, (2 * BLK, BLK), 0) % BLK
    c_i = lax.broadcasted_iota(jnp.int32, (2 * BLK, BLK), 1)
    keep = c_i <= r_i

    def logits(j, diag):
        off = pl.multiple_of(j * BLK, BLK)
        out = []
        for p in pairs:
            s = _dot_nt(qs[p], k_ref[0, pl.ds(off, BLK), ps[p]])
            bias = cref[p] - c_ref[0, p, j]
            s = jnp.concatenate([s[:BLK] + bias[0:1], s[BLK:] + bias[1:2]], axis=0)
            out.append(jnp.where(keep, s, NEG) if diag else s)
        return out

    def max_step(j, mt, diag=False):
        s = logits(j, diag)
        return tuple(jnp.maximum(mt[p], s[p]) for p in pairs)

    mt = lax.fori_loop(0, i, max_step, tuple(jnp.full((2 * BLK, BLK), NEG, F32) for _ in pairs))
    mt = max_step(i, mt, True)
    mrow = [jnp.broadcast_to(jnp.max(mt[p], axis=1, keepdims=True), (2 * BLK, BLK)) for p in pairs]

    def acc_step(j, acc, diag=False):
        off = pl.multiple_of(j * BLK, BLK)
        s = logits(j, diag)
        pr = [jnp.exp(s[p] - mrow[p]).astype(BF16) for p in pairs]
        new = []
        for p in pairs:
            v2 = v_ref[0, pl.ds(off, BLK), ps[p]]
            new.append(acc[2 * p] + _dot(pr[p][:BLK], jnp.where(is_a, v2, one)))
            new.append(acc[2 * p + 1] + _dot(pr[p][BLK:], jnp.where(is_a, one, v2)))
        return tuple(new)

    acc = lax.fori_loop(0, i, acc_step, tuple(jnp.zeros((BLK, LANES), F32) for _ in range(2 * n_p)))
    acc = acc_step(i, acc, True)
    for p in pairs:
        o_a = acc[2 * p] / pltpu.roll(acc[2 * p], HEAD, axis=1)
        o_b = acc[2 * p + 1] / pltpu.roll(acc[2 * p + 1], HEAD, axis=1)
        g = g_ref[0, :, ps[p]].astype(F32)
        o_ref[0, :, ps[p]] = (jnp.where(is_a, o_a, o_b) * (g * _sigmoid(g))).astype(BF16)


def _fox_attn(p3, cum5):
    B, L, _ = p3.shape
    n_pair = (p3.shape[2] // 4) // LANES
    nq = L // BLK
    n_p = ATTN_PAIRS
    width = n_p * LANES
    n_grp = n_pair // n_p
    return pl.pallas_call(
        _fox_attn_kernel,
        grid=(B, n_grp, nq),
        in_specs=[
            pl.BlockSpec((1, BLK, width), lambda b, p, i: (b, i, p)),
            pl.BlockSpec((1, L, width), lambda b, p, i: (b, 0, n_grp + p)),
            pl.BlockSpec((1, L, width), lambda b, p, i: (b, 0, 2 * n_grp + p)),
            pl.BlockSpec((1, BLK, width), lambda b, p, i: (b, i, 3 * n_grp + p)),
            pl.BlockSpec((1, n_p, nq, 2, BLK), lambda b, p, i: (b, p, 0, 0, 0)),
        ],
        out_specs=pl.BlockSpec((1, BLK, width), lambda b, p, i: (b, i, p)),
        out_shape=jax.ShapeDtypeStruct((B, L, n_pair * LANES), BF16),
        compiler_params=pltpu.CompilerParams(
            dimension_semantics=("arbitrary", "arbitrary", "arbitrary"), vmem_limit_bytes=VMEM_LIMIT),
        name="fox_attn",
    )(p3, p3, p3, p3, cum5)


def _head_sums(x):
    row = lax.broadcasted_iota(jnp.int32, (LANES, LANES), 0)
    col = lax.broadcasted_iota(jnp.int32, (LANES, LANES), 1)
    ones = ((row // HEAD) == (col // HEAD)).astype(BF16)
    parts = []
    for g in range(x.shape[1] // LANES):
        parts.append(_dot(x[:, g * LANES:(g + 1) * LANES].astype(BF16), ones))
    return jnp.concatenate(parts, axis=1)


def _wkv_kernel(r_ref, k_ref, v_ref, g_ref, rh_ref, kh_ref, vh_ref, gh_ref, t_ref, th_ref,
                mu_ref, mut_ref, w0_ref, wup_ref, a0_ref, aup_ref, kk_ref, ka_ref, rk_ref,
                lnw_ref, lnb_ref, o_ref, st_ref):
    c = pl.program_id(1)

    @pl.when(c == 0)
    def _():
        st_ref[...] = jnp.zeros_like(st_ref)

    n_b = r_ref.shape[0]
    D = r_ref.shape[2]
    n_pair = D // LANES
    not_first = c != 0
    row = lax.broadcasted_iota(jnp.int32, (CHUNK, CHUNK), 0)
    col = lax.broadcasted_iota(jnp.int32, (CHUNK, CHUNK), 1)
    sub_diag = (col + 1 == row).astype(BF16)
    tril = (col <= row).astype(BF16)
    top0 = (lax.broadcasted_iota(jnp.int32, (8, 1), 0) == 0) & not_first
    trow0 = lax.broadcasted_iota(jnp.int32, (CHUNK, 1), 0) == 0

    def lerp(cur, sh, prev, mu):
        sh = jnp.concatenate([sh[0:8] + jnp.where(top0, prev, 0.0), sh[8:]], axis=0)
        return cur + (sh - cur) * mu

    def shifted(cur, prev, mu):
        return lerp(cur.astype(F32), _dot(sub_diag, cur), prev.astype(F32), mu)

    def prep(b):
        r = shifted(r_ref[b], rh_ref[b, 7:8, :], mu_ref[:, 0:D])
        k = shifted(k_ref[b], kh_ref[b, 7:8, :], mu_ref[:, D:2 * D])
        v = shifted(v_ref[b], vh_ref[b, 7:8, :], mu_ref[:, 2 * D:3 * D])
        g = shifted(g_ref[b], gh_ref[b, 7:8, :], mu_ref[:, 3 * D:4 * D])
        tcur = t_ref[b]
        tl = lerp(tcur, jnp.where(trow0, 0.0, pltpu.roll(tcur, 1, axis=0)), th_ref[b, 7:8, :],
                  mut_ref[...])
        w_log = w0_ref[...] + _dot(jnp.tanh(tl[:, :HEAD]).astype(BF16), wup_ref[...])
        lw = (-0.5 * DECAY_SCALE) * jnp.tanh(0.5 * w_log) - 0.5 * DECAY_SCALE
        a = _sigmoid(a0_ref[...] + _dot(tl[:, HEAD:].astype(BF16), aup_ref[...]))
        kk = k * kk_ref[...]
        kk = kk * lax.rsqrt(jnp.maximum(_head_sums(kk * kk), 1e-24))
        k = k * (1.0 + (a - 1.0) * ka_ref[...])
        hi, mid, lo = _split3(lw)
        cl = _dot(tril, hi) + _dot(tril, mid) + _dot(tril, lo)
        e_neg = jnp.exp(-cl)
        return dict(
            r=r, k=k, v=v, g=g,
            r_t=(r * jnp.exp(cl)).astype(BF16),
            a_t=(-kk * jnp.exp(cl - lw)).astype(BF16),
            b_t=(kk * a * e_neg).astype(BF16),
            k_t=(k * e_neg).astype(BF16),
            w_end=jnp.exp(cl[CHUNK - 1:CHUNK, :]),
            v_b=v.astype(BF16))

    pre = [prep(b) for b in range(n_b)]

    lane = lax.broadcasted_iota(jnp.int32, (1, LANES), 1)
    is_a = lane < HEAD
    trw = lax.broadcasted_iota(jnp.int32, (CHUNK, LANES), 0)
    jw = lax.broadcasted_iota(jnp.int32, (CHUNK, LANES), 1) % HEAD
    strict = jw < trw
    incl = jw <= trw
    r2 = lax.broadcasted_iota(jnp.int32, (LANES, LANES), 0) // HEAD
    c2 = lax.broadcasted_iota(jnp.int32, (LANES, LANES), 1) // HEAD
    same_head = r2 == c2
    zb16 = jnp.zeros((), BF16)

    def bdiag(xw, swap=False):
        za = jnp.where(is_a, xw, jnp.zeros_like(xw))
        zb = jnp.where(is_a, jnp.zeros_like(xw), xw)
        return jnp.concatenate([zb, za] if swap else [za, zb], axis=0)

    streams = [(b, p) for b in range(n_b) for p in range(n_pair)]
    ns = range(len(streams))

    def part(name, n):
        b, p = streams[n]
        return pre[b][name][:, p * LANES:(p + 1) * LANES]

    xq = [jnp.concatenate([part("a_t", n), part("r_t", n)], axis=0) for n in ns]
    ycat = [jnp.concatenate(
        [jnp.where(is_a, jnp.concatenate([part("k_t", n), part("b_t", n)], axis=0), zb16),
         jnp.where(is_a, zb16, jnp.concatenate([part("b_t", n), part("k_t", n)], axis=0))],
        axis=0) for n in ns]
    sc = [_dot_nt(xq[n], ycat[n]) for n in ns]
    s0 = [st_ref[b, p] for b, p in streams]
    xs = [_dot_nt(xq[n], s0[n].astype(BF16)) for n in ns]
    top_lo = [sc[n][:CHUNK, :LANES] for n in ns]
    top_hi = [sc[n][:CHUNK, LANES:] for n in ns]
    bot_lo = [sc[n][CHUNK:, :LANES] for n in ns]
    bot_hi = [sc[n][CHUNK:, LANES:] for n in ns]
    a_ak = [jnp.where(strict, jnp.where(is_a, top_lo[n], top_hi[n]), 0.0).astype(BF16) for n in ns]
    a_ab = [jnp.where(strict, jnp.where(is_a, top_hi[n], top_lo[n]), 0.0) for n in ns]
    m_rk = [jnp.where(incl, jnp.where(is_a, bot_lo[n], bot_hi[n]), 0.0).astype(BF16) for n in ns]
    m_rb = [jnp.where(incl, jnp.where(is_a, bot_hi[n], bot_lo[n]), 0.0).astype(BF16) for n in ns]

    eye = (jw == trw).astype(F32)
    lvl1 = (trw // 2 == jw // 2) & strict
    tw = [eye + jnp.where(lvl1, a_ab[n], 0.0) for n in ns]
    s = 2
    while s < CHUNK:
        off = (trw // (2 * s) == jw // (2 * s)) & ((trw // s) % 2 == 1) & ((jw // s) % 2 == 0)
        tb = [t.astype(BF16) for t in tw]
        pm = [_dot(tb[n], bdiag(jnp.where(off, a_ab[n], 0.0).astype(BF16))) for n in ns]
        tw = [tw[n] + _dot(pm[n].astype(BF16), bdiag(tb[n])) for n in ns]
        s *= 2

    vbd = [bdiag(part("v_b", n)) for n in ns]
    gm = [xs[n][:CHUNK] + _dot(a_ak[n], vbd[n]) for n in ns]
    u = [_dot(tw[n].astype(BF16), bdiag(gm[n].astype(BF16), swap=True)).astype(BF16) for n in ns]
    ys = [xs[n][CHUNK:] + _dot(jnp.concatenate([m_rb[n], m_rk[n]], axis=1),
                               jnp.concatenate([bdiag(u[n], swap=True), vbd[n]], axis=0))
          for n in ns]
    for n in ns:
        b, p = streams[n]
        uv = jnp.concatenate([u[n], part("v_b", n)], axis=0)
        bk = jnp.concatenate([part("b_t", n), part("k_t", n)], axis=0)
        st_ref[b, p] = (s0[n] + jnp.where(same_head, _dot_tn(uv, bk), 0.0)) * part("w_end", n)

    inv_n = 1.0 / HEAD
    for b in range(n_b):
        y = jnp.concatenate(ys[b * n_pair:(b + 1) * n_pair], axis=1)
        q = pre[b]
        mean = _head_sums(y) * inv_n
        d = y - mean
        var = _head_sums(d * d) * inv_n
        yn = d * lax.rsqrt(var + GN_EPS) * lnw_ref[...] + lnb_ref[...]
        bonus = _head_sums(q["r"] * q["k"] * rk_ref[...]) * q["v"]
        o_ref[b] = ((yn + bonus) * (q["g"] * _sigmoid(q["g"]))).astype(BF16)


def _wkv(p3, tail3, mu, mu_t, w0, w_up, a0, a_up, k_k, k_a, r_k, ln_w, ln_b):
    B, L, n_main = p3.shape
    D = n_main // 4
    nc = L // CHUNK
    sub = CHUNK // 8
    n_b = WKV_BATCH if B % WKV_BATCH == 0 else 1

    def cur(s):
        return pl.BlockSpec((n_b, CHUNK, D), lambda b, c: (b, c, s))

    def halo(s):
        return pl.BlockSpec((n_b, 8, D), lambda b, c: (b, jnp.maximum(c * sub - 1, 0), s))

    def vec(n):
        return pl.BlockSpec((1, n), lambda b, c: (0, 0))

    lora = pl.BlockSpec((HEAD, D), lambda b, c: (0, 0))
    return pl.pallas_call(
        _wkv_kernel,
        grid=(B // n_b, nc),
        in_specs=[cur(0), cur(1), cur(2), cur(3), halo(0), halo(1), halo(2), halo(3),
                  pl.BlockSpec((n_b, CHUNK, LANES), lambda b, c: (b, c, 0)),
                  pl.BlockSpec((n_b, 8, LANES), lambda b, c: (b, jnp.maximum(c * sub - 1, 0), 0)),
                  vec(n_main), vec(LANES), vec(D), lora, vec(D), lora,
                  vec(D), vec(D), vec(D), vec(D), vec(D)],
        out_specs=pl.BlockSpec((n_b, CHUNK, D), lambda b, c: (b, c, 0)),
        out_shape=jax.ShapeDtypeStruct((B, L, D), BF16),
        scratch_shapes=[pltpu.VMEM((n_b, D // LANES, LANES, LANES), F32)],
        compiler_params=pltpu.CompilerParams(
            dimension_semantics=("arbitrary", "arbitrary"), vmem_limit_bytes=VMEM_LIMIT),
        name="wkv7",
    )(p3, p3, p3, p3, p3, p3, p3, p3, tail3, tail3,
      mu, mu_t, w0, w_up, a0, a_up, k_k, k_a, r_k, ln_w, ln_b)


def _fox_layer(h, B, L, g_pre, g_post, w_in, b_f, w_out):
    D = h.shape[1]
    n_heads = b_f.shape[0]
    n_main = 4 * D
    w_cat = jnp.concatenate(
        [w_in, jnp.zeros((D, LANES - n_heads), w_in.dtype)], axis=1).astype(BF16)
    p, tail = _norm_proj(h, g_pre.reshape(1, D), w_cat, q_cols=D, q_scale=HEAD ** -0.5)
    f_t = tail[:, :n_heads].reshape(B, L, n_heads).transpose(0, 2, 1).reshape(B * n_heads, L)
    b_col = jnp.tile(b_f, B).reshape(B * n_heads, 1)
    cum = _fox_cum(f_t, b_col)
    cum5 = cum.reshape(B, n_heads // 2, 2, L // BLK, BLK).transpose(0, 1, 3, 2, 4)
    z = _fox_attn(p.reshape(B, L, n_main), cum5)
    return _out_proj(z.reshape(B * L, D), w_out.astype(BF16), h, g_post.reshape(1, D))


def _rwkv_layer(h, B, L, g_pre, g_post, w_in, mu, w0, w_up, a0, a_up, k_k, k_a, r_k,
                ln_w, ln_b, w_out):
    D = h.shape[1]
    n_main = 4 * D
    p, tail = _norm_proj(h, g_pre.reshape(1, D), w_in.astype(BF16))
    row = lambda t: t.reshape(1, -1)
    z = _wkv(p.reshape(B, L, n_main), tail.reshape(B, L, LANES),
             row(mu[:n_main]), row(mu[n_main:]), row(w0), w_up.astype(BF16), row(a0),
             a_up.astype(BF16), row(k_k), row(k_a), row(r_k), row(ln_w), row(ln_b))
    return _out_proj(z.reshape(B * L, D), w_out.astype(BF16), h, g_post.reshape(1, D))


def kernel(x, meta_tokens, norm_pre, norm_post, fox_w_in, fox_b_f, fox_w_out, rwkv_w_in, rwkv_mu, rwkv_w0, rwkv_w_up, rwkv_a0, rwkv_a_up, rwkv_k_k, rwkv_k_a, rwkv_r_k, rwkv_ln_w, rwkv_ln_b, rwkv_w_out):
    B, S, D = x.shape
    depth = norm_pre.shape[0]
    L0 = N_META + S
    L = -(-L0 // BLK) * BLK
    assert (B * L) % ROW_TILE == 0 and D % LANES == 0
    meta = jnp.broadcast_to(meta_tokens[None].astype(x.dtype), (B, N_META, D))
    h = jnp.concatenate([meta, x, jnp.zeros((B, L - L0, D), x.dtype)], axis=1).reshape(B * L, D)
    for i in range(depth):
        j = i // 2
        if i % 2 == 0:
            h = _fox_layer(h, B, L, norm_pre[i], norm_post[i], fox_w_in[j], fox_b_f[j], fox_w_out[j])
        else:
            h = _rwkv_layer(h, B, L, norm_pre[i], norm_post[i], rwkv_w_in[j], rwkv_mu[j],
                            rwkv_w0[j], rwkv_w_up[j], rwkv_a0[j], rwkv_a_up[j], rwkv_k_k[j],
                            rwkv_k_a[j], rwkv_r_k[j], rwkv_ln_w[j], rwkv_ln_b[j], rwkv_w_out[j])
    return h.reshape(B, L, D)[:, N_META:L0]
```

```python
import functools
import math

import jax
import jax.numpy as jnp
from jax import lax
from jax.experimental import pallas as pl
from jax.experimental.pallas import tpu as pltpu

F32 = jnp.float32
BF16 = jnp.bfloat16

N_META = 16
HEAD = 64
LANES = 128
BLK = 128
ATTN_PAIRS = 8
ATTN_SUBGROUP = 2
KV_GROUP = 4
CHUNK = 64
WKV_BATCH = 2
ROW_TILE = 512
COL_TILE = 512
NORM_EPS = 1e-6
GN_EPS = 64e-5
DECAY_SCALE = math.exp(-0.5)
LOG2E = math.log2(math.e)
NEG = -1e30
VMEM_LIMIT = 48 * 1024 * 1024


def _dot(a, b):
    return jnp.dot(a, b, preferred_element_type=F32)


def _dot_nt(a, b):
    return lax.dot_general(a, b, (((1,), (1,)), ((), ())), preferred_element_type=F32)


def _dot_tn(a, b):
    return lax.dot_general(a, b, (((0,), (0,)), ((), ())), preferred_element_type=F32)


def _split3(x):
    hi = x.astype(BF16)
    r1 = x - hi.astype(F32)
    mid = r1.astype(BF16)
    lo = (r1 - mid.astype(F32)).astype(BF16)
    return hi, mid, lo


def _sigmoid(x):
    return 0.5 * jnp.tanh(0.5 * x) + 0.5


def _norm_proj_kernel(h_ref, g_ref, w_ref, o_ref, t_ref, *, n_main, q_cols, q_scale):
    x = h_ref[...]
    var = jnp.mean(x * x, axis=-1, keepdims=True)
    u = (x * lax.rsqrt(var + NORM_EPS) * g_ref[...]).astype(BF16)
    for j in range(0, n_main, COL_TILE):
        acc = _dot(u, w_ref[:, j:j + COL_TILE])
        if j < q_cols:
            acc = acc * q_scale
        o_ref[:, j:j + COL_TILE] = acc.astype(BF16)
    t_ref[...] = _dot(u, w_ref[:, n_main:])


def _norm_proj(h, g, w, *, q_cols=0, q_scale=1.0):
    T, D = h.shape
    n_all = w.shape[1]
    n_main = n_all - LANES
    kern = functools.partial(_norm_proj_kernel, n_main=n_main, q_cols=q_cols, q_scale=q_scale)
    return pl.pallas_call(
        kern,
        grid=(T // ROW_TILE,),
        in_specs=[
            pl.BlockSpec((ROW_TILE, D), lambda i: (i, 0)),
            pl.BlockSpec((1, D), lambda i: (0, 0)),
            pl.BlockSpec((D, n_all), lambda i: (0, 0)),
        ],
        out_specs=[
            pl.BlockSpec((ROW_TILE, n_main), lambda i: (i, 0)),
            pl.BlockSpec((ROW_TILE, LANES), lambda i: (i, 0)),
        ],
        out_shape=[
            jax.ShapeDtypeStruct((T, n_main), BF16),
            jax.ShapeDtypeStruct((T, LANES), F32),
        ],
        compiler_params=pltpu.CompilerParams(
            dimension_semantics=("arbitrary",), vmem_limit_bytes=VMEM_LIMIT),
        name="norm_proj",
    )(h, g, w)


def _out_proj_kernel(z_ref, w_ref, h_ref, g_ref, o_ref):
    m = _dot(z_ref[...], w_ref[...])
    var = jnp.mean(m * m, axis=-1, keepdims=True)
    o_ref[...] = h_ref[...] + m * lax.rsqrt(var + NORM_EPS) * g_ref[...]


def _out_proj(z, w, h, g):
    T, D = h.shape
    return pl.pallas_call(
        _out_proj_kernel,
        grid=(T // ROW_TILE,),
        in_specs=[
            pl.BlockSpec((ROW_TILE, D), lambda i: (i, 0)),
            pl.BlockSpec((D, D), lambda i: (0, 0)),
            pl.BlockSpec((ROW_TILE, D), lambda i: (i, 0)),
            pl.BlockSpec((1, D), lambda i: (0, 0)),
        ],
        out_specs=pl.BlockSpec((ROW_TILE, D), lambda i: (i, 0)),
        out_shape=jax.ShapeDtypeStruct((T, D), F32),
        compiler_params=pltpu.CompilerParams(
            dimension_semantics=("arbitrary",), vmem_limit_bytes=VMEM_LIMIT),
        name="out_proj",
    )(z, w, h, g)


def _cum_kernel(x_ref, b_ref, o_ref, carry_ref):
    j = pl.program_id(0)

    @pl.when(j == 0)
    def _():
        carry_ref[...] = jnp.zeros_like(carry_ref)

    x = x_ref[...] + b_ref[...]
    lf = jnp.minimum(x, 0.0) - jnp.log(1.0 + jnp.exp(-jnp.abs(x)))
    row = lax.broadcasted_iota(jnp.int32, (BLK, BLK), 0)
    col = lax.broadcasted_iota(jnp.int32, (BLK, BLK), 1)
    tri = (row <= col).astype(BF16)
    hi, mid, lo = _split3(lf)
    out = _dot(hi, tri) + _dot(mid, tri) + _dot(lo, tri) + carry_ref[...]
    o_ref[...] = out
    carry_ref[...] = out[:, BLK - 1:BLK]


def _fox_cum(f_t, b_col):
    R, L = f_t.shape
    return pl.pallas_call(
        _cum_kernel,
        grid=(L // BLK,),
        in_specs=[
            pl.BlockSpec((R, BLK), lambda j: (0, j)),
            pl.BlockSpec((R, 1), lambda j: (0, 0)),
        ],
        out_specs=pl.BlockSpec((R, BLK), lambda j: (0, j)),
        out_shape=jax.ShapeDtypeStruct((R, L), F32),
        scratch_shapes=[pltpu.VMEM((R, 1), F32)],
        compiler_params=pltpu.CompilerParams(dimension_semantics=("arbitrary",)),
        name="fox_cum",
    )(f_t, b_col)


def _fox_attn_kernel(q_ref, k_ref, v_ref, g_ref, c_ref, o_ref):
    i = pl.program_id(2)
    n_p = q_ref.shape[2] // LANES
    pairs = range(n_p)
    ps = [slice(p * LANES, (p + 1) * LANES) for p in pairs]
    lane = lax.broadcasted_iota(jnp.int32, (1, LANES), 1)
    is_a = lane < HEAD
    zq = jnp.zeros((), BF16)
    one = jnp.ones((), BF16)
    qs = [jnp.concatenate([jnp.where(is_a, q_ref[0, :, s], zq), jnp.where(is_a, zq, q_ref[0, :, s])],
                          axis=0) for s in ps]
    cref = [c_ref[0, p, i][:, 0:1] for p in pairs]

    r_i = lax.broadcasted_iota(jnp.int32, (2 * BLK, BLK), 0) % BLK
    c_i = lax.broadcasted_iota(jnp.int32, (2 * BLK, BLK), 1)
    keep = c_i <= r_i

    def step(j, carry, nb=1):
        m, acc = carry
        off = pl.multiple_of(j * BLK, BLK)
        unmasked = jnp.logical_or(keep, j + (nb - 1) != i)
        mn, new_acc = [None] * n_p, [None] * (2 * n_p)
        n_sub = ATTN_SUBGROUP if nb > 1 else n_p
        for g0 in range(0, n_p, n_sub):
            sub = range(g0, min(g0 + n_sub, n_p))
            tiles = {}
            for p in sub:
                s = _dot_nt(qs[p], k_ref[0, pl.ds(off, nb * BLK), ps[p]])
                tp = []
                for t in range(nb):
                    bias = (cref[p] - c_ref[0, p, j + t]) * LOG2E
                    st = s[:, t * BLK:(t + 1) * BLK]
                    st = jnp.concatenate([st[:BLK] + bias[0:1], st[BLK:] + bias[1:2]], axis=0)
                    tp.append(jnp.where(unmasked, st, NEG) if t == nb - 1 else st)
                tiles[p] = tp
            for p in sub:
                mn[p] = jnp.maximum(
                    m[p], jnp.max(functools.reduce(jnp.maximum, tiles[p]), axis=1, keepdims=True))
            for p in sub:
                pr = jnp.concatenate([jnp.exp2(st - mn[p]).astype(BF16) for st in tiles[p]], axis=1)
                alpha = jnp.exp2(m[p] - mn[p])
                v2 = v_ref[0, pl.ds(off, nb * BLK), ps[p]]
                new_acc[2 * p] = acc[2 * p] * alpha[:BLK] + _dot(pr[:BLK], jnp.where(is_a, v2, one))
                new_acc[2 * p + 1] = (acc[2 * p + 1] * alpha[BLK:]
                                      + _dot(pr[BLK:], jnp.where(is_a, one, v2)))
        return tuple(mn), tuple(new_acc)

    n_grp = (i + 1) // KV_GROUP
    carry = (tuple(jnp.full((2 * BLK, 1), NEG, F32) for _ in pairs),
             tuple(jnp.zeros((BLK, LANES), F32) for _ in range(2 * n_p)))
    carry = lax.fori_loop(0, n_grp, lambda jj, c: step(jj * KV_GROUP, c, KV_GROUP), carry)
    _, acc = lax.fori_loop(n_grp * KV_GROUP, i + 1, step, carry)
    for p in pairs:
        o_a = acc[2 * p] / pltpu.roll(acc[2 * p], HEAD, axis=1)
        o_b = acc[2 * p + 1] / pltpu.roll(acc[2 * p + 1], HEAD, axis=1)
        g = g_ref[0, :, ps[p]].astype(F32)
        o_ref[0, :, ps[p]] = (jnp.where(is_a, o_a, o_b) * (g * _sigmoid(g))).astype(BF16)


def _fox_attn(p3, cum5):
    B, L, _ = p3.shape
    n_pair = (p3.shape[2] // 4) // LANES
    nq = L // BLK
    n_p = ATTN_PAIRS
    width = n_p * LANES
    n_grp = n_pair // n_p
    return pl.pallas_call(
        _fox_attn_kernel,
        grid=(B, n_grp, nq),
        in_specs=[
            pl.BlockSpec((1, BLK, width), lambda b, p, i: (b, i, p)),
            pl.BlockSpec((1, L, width), lambda b, p, i: (b, 0, n_grp + p)),
            pl.BlockSpec((1, L, width), lambda b, p, i: (b, 0, 2 * n_grp + p)),
            pl.BlockSpec((1, BLK, width), lambda b, p, i: (b, i, 3 * n_grp + p)),
            pl.BlockSpec((1, n_p, nq, 2, BLK), lambda b, p, i: (b, p, 0, 0, 0)),
        ],
        out_specs=pl.BlockSpec((1, BLK, width), lambda b, p, i: (b, i, p)),
        out_shape=jax.ShapeDtypeStruct((B, L, n_pair * LANES), BF16),
        compiler_params=pltpu.CompilerParams(
            dimension_semantics=("arbitrary", "arbitrary", "arbitrary"), vmem_limit_bytes=VMEM_LIMIT),
        name="fox_attn",
    )(p3, p3, p3, p3, cum5)


def _head_sums(x):
    row = lax.broadcasted_iota(jnp.int32, (LANES, LANES), 0)
    col = lax.broadcasted_iota(jnp.int32, (LANES, LANES), 1)
    ones = ((row // HEAD) == (col // HEAD)).astype(BF16)
    parts = []
    for g in range(x.shape[1] // LANES):
        parts.append(_dot(x[:, g * LANES:(g + 1) * LANES].astype(BF16), ones))
    return jnp.concatenate(parts, axis=1)


def _wkv_kernel(r_ref, k_ref, v_ref, g_ref, rh_ref, kh_ref, vh_ref, gh_ref, t_ref, th_ref,
                mu_ref, mut_ref, w0_ref, wup_ref, a0_ref, aup_ref, kk_ref, ka_ref, rk_ref,
                lnw_ref, lnb_ref, o_ref, st_ref):
    c = pl.program_id(1)

    @pl.when(c == 0)
    def _():
        st_ref[...] = jnp.zeros_like(st_ref)

    n_b = r_ref.shape[0]
    D = r_ref.shape[2]
    n_pair = D // LANES
    not_first = c != 0
    row = lax.broadcasted_iota(jnp.int32, (CHUNK, CHUNK), 0)
    col = lax.broadcasted_iota(jnp.int32, (CHUNK, CHUNK), 1)
    sub_diag = (col + 1 == row).astype(BF16)
    tril = (col <= row).astype(BF16)
    top0 = (lax.broadcasted_iota(jnp.int32, (8, 1), 0) == 0) & not_first
    trow0 = lax.broadcasted_iota(jnp.int32, (CHUNK, 1), 0) == 0

    def lerp(cur, sh, prev, mu):
        sh = jnp.concatenate([sh[0:8] + jnp.where(top0, prev, 0.0), sh[8:]], axis=0)
        return cur + (sh - cur) * mu

    def shifted(cur, prev, mu):
        return lerp(cur.astype(F32), _dot(sub_diag, cur), prev.astype(F32), mu)

    def prep(b):
        r = shifted(r_ref[b], rh_ref[b, 7:8, :], mu_ref[:, 0:D])
        k = shifted(k_ref[b], kh_ref[b, 7:8, :], mu_ref[:, D:2 * D])
        v = shifted(v_ref[b], vh_ref[b, 7:8, :], mu_ref[:, 2 * D:3 * D])
        g = shifted(g_ref[b], gh_ref[b, 7:8, :], mu_ref[:, 3 * D:4 * D])
        tcur = t_ref[b]
        tl = lerp(tcur, jnp.where(trow0, 0.0, pltpu.roll(tcur, 1, axis=0)), th_ref[b, 7:8, :],
                  mut_ref[...])
        w_log = w0_ref[...] + _dot(jnp.tanh(tl[:, :HEAD]).astype(BF16), wup_ref[...])
        lw = (-0.5 * DECAY_SCALE * LOG2E) * jnp.tanh(0.5 * w_log) - 0.5 * DECAY_SCALE * LOG2E
        a = _sigmoid(a0_ref[...] + _dot(tl[:, HEAD:].astype(BF16), aup_ref[...]))
        kk = k * kk_ref[...]
        kk = kk * lax.rsqrt(jnp.maximum(_head_sums(kk * kk), 1e-24))
        k = k * (1.0 + (a - 1.0) * ka_ref[...])
        hi, mid, lo = _split3(lw)
        cl = _dot(tril, hi) + _dot(tril, mid) + _dot(tril, lo)
        e_neg = jnp.exp2(-cl)
        return dict(
            r=r, k=k, v=v, g=g,
            r_t=(r * jnp.exp2(cl)).astype(BF16),
            a_t=(-kk * jnp.exp2(cl - lw)).astype(BF16),
            b_t=(kk * a * e_neg).astype(BF16),
            k_t=(k * e_neg).astype(BF16),
            w_end=jnp.exp2(cl[CHUNK - 1:CHUNK, :]),
            v_b=v.astype(BF16))

    pre = [prep(b) for b in range(n_b)]

    lane = lax.broadcasted_iota(jnp.int32, (1, LANES), 1)
    is_a = lane < HEAD
    trw = lax.broadcasted_iota(jnp.int32, (CHUNK, LANES), 0)
    jw = lax.broadcasted_iota(jnp.int32, (CHUNK, LANES), 1) % HEAD
    strict = jw < trw
    incl = jw <= trw
    r2 = lax.broadcasted_iota(jnp.int32, (LANES, LANES), 0) // HEAD
    c2 = lax.broadcasted_iota(jnp.int32, (LANES, LANES), 1) // HEAD
    same_head = r2 == c2
    zb16 = jnp.zeros((), BF16)

    def bdiag(xw, swap=False):
        za = jnp.where(is_a, xw, jnp.zeros_like(xw))
        zb = jnp.where(is_a, jnp.zeros_like(xw), xw)
        return jnp.concatenate([zb, za] if swap else [za, zb], axis=0)

    streams = [(b, p) for b in range(n_b) for p in range(n_pair)]
    ns = range(len(streams))

    def part(name, n):
        b, p = streams[n]
        return pre[b][name][:, p * LANES:(p + 1) * LANES]

    xq = [jnp.concatenate([part("a_t", n), part("r_t", n)], axis=0) for n in ns]
    ycat = [jnp.concatenate(
        [jnp.where(is_a, jnp.concatenate([part("k_t", n), part("b_t", n)], axis=0), zb16),
         jnp.where(is_a, zb16, jnp.concatenate([part("b_t", n), part("k_t", n)], axis=0))],
        axis=0) for n in ns]
    sc = [_dot_nt(xq[n], ycat[n]) for n in ns]
    s0 = [st_ref[b, p] for b, p in streams]
    xs = [_dot_nt(xq[n], s0[n].astype(BF16)) for n in ns]
    top_lo = [sc[n][:CHUNK, :LANES] for n in ns]
    top_hi = [sc[n][:CHUNK, LANES:] for n in ns]
    bot_lo = [sc[n][CHUNK:, :LANES] for n in ns]
    bot_hi = [sc[n][CHUNK:, LANES:] for n in ns]
    a_ak = [jnp.where(strict, jnp.where(is_a, top_lo[n], top_hi[n]), 0.0).astype(BF16) for n in ns]
    a_ab = [jnp.where(strict, jnp.where(is_a, top_hi[n], top_lo[n]), 0.0) for n in ns]
    m_rk = [jnp.where(incl, jnp.where(is_a, bot_lo[n], bot_hi[n]), 0.0).astype(BF16) for n in ns]
    m_rb = [jnp.where(incl, jnp.where(is_a, bot_hi[n], bot_lo[n]), 0.0).astype(BF16) for n in ns]

    eye = (jw == trw).astype(F32)
    lvl1 = (trw // 2 == jw // 2) & strict
    tw = [eye + jnp.where(lvl1, a_ab[n], 0.0) for n in ns]
    s = 2
    while s < CHUNK:
        off = (trw // (2 * s) == jw // (2 * s)) & ((trw // s) % 2 == 1) & ((jw // s) % 2 == 0)
        tb = [t.astype(BF16) for t in tw]
        pm = [_dot(tb[n], bdiag(jnp.where(off, a_ab[n], 0.0).astype(BF16))) for n in ns]
        tw = [tw[n] + _dot(pm[n].astype(BF16), bdiag(tb[n])) for n in ns]
        s *= 2

    vbd = [bdiag(part("v_b", n)) for n in ns]
    gm = [xs[n][:CHUNK] + _dot(a_ak[n], vbd[n]) for n in ns]
    u = [_dot(tw[n].astype(BF16), bdiag(gm[n].astype(BF16), swap=True)).astype(BF16) for n in ns]
    ys = [xs[n][CHUNK:] + _dot(jnp.concatenate([m_rb[n], m_rk[n]], axis=1),
                               jnp.concatenate([bdiag(u[n], swap=True), vbd[n]], axis=0))
          for n in ns]
    for n in ns:
        b, p = streams[n]
        uv = jnp.concatenate([u[n], part("v_b", n)], axis=0)
        bk = jnp.concatenate([part("b_t", n), part("k_t", n)], axis=0)
        st_ref[b, p] = (s0[n] + jnp.where(same_head, _dot_tn(uv, bk), 0.0)) * part("w_end", n)

    inv_n = 1.0 / HEAD
    for b in range(n_b):
        y = jnp.concatenate(ys[b * n_pair:(b + 1) * n_pair], axis=1)
        q = pre[b]
        mean = _head_sums(y) * inv_n
        d = y - mean
        var = _head_sums(d * d) * inv_n
        yn = d * lax.rsqrt(var + GN_EPS) * lnw_ref[...] + lnb_ref[...]
        bonus = _head_sums(q["r"] * q["k"] * rk_ref[...]) * q["v"]
        o_ref[b] = ((yn + bonus) * (q["g"] * _sigmoid(q["g"]))).astype(BF16)


def _wkv(p3, tail3, mu, mu_t, w0, w_up, a0, a_up, k_k, k_a, r_k, ln_w, ln_b):
    B, L, n_main = p3.shape
    D = n_main // 4
    nc = L // CHUNK
    sub = CHUNK // 8
    n_b = WKV_BATCH if B % WKV_BATCH == 0 else 1

    def cur(s):
        return pl.BlockSpec((n_b, CHUNK, D), lambda b, c: (b, c, s))

    def halo(s):
        return pl.BlockSpec((n_b, 8, D), lambda b, c: (b, jnp.maximum(c * sub - 1, 0), s))

    def vec(n):
        return pl.BlockSpec((1, n), lambda b, c: (0, 0))

    lora = pl.BlockSpec((HEAD, D), lambda b, c: (0, 0))
    return pl.pallas_call(
        _wkv_kernel,
        grid=(B // n_b, nc),
        in_specs=[cur(0), cur(1), cur(2), cur(3), halo(0), halo(1), halo(2), halo(3),
                  pl.BlockSpec((n_b, CHUNK, LANES), lambda b, c: (b, c, 0)),
                  pl.BlockSpec((n_b, 8, LANES), lambda b, c: (b, jnp.maximum(c * sub - 1, 0), 0)),
                  vec(n_main), vec(LANES), vec(D), lora, vec(D), lora,
                  vec(D), vec(D), vec(D), vec(D), vec(D)],
        out_specs=pl.BlockSpec((n_b, CHUNK, D), lambda b, c: (b, c, 0)),
        out_shape=jax.ShapeDtypeStruct((B, L, D), BF16),
        scratch_shapes=[pltpu.VMEM((n_b, D // LANES, LANES, LANES), F32)],
        compiler_params=pltpu.CompilerParams(
            dimension_semantics=("arbitrary", "arbitrary"), vmem_limit_bytes=VMEM_LIMIT),
        name="wkv7",
    )(p3, p3, p3, p3, p3, p3, p3, p3, tail3, tail3,
      mu, mu_t, w0, w_up, a0, a_up, k_k, k_a, r_k, ln_w, ln_b)


def _fox_layer(h, B, L, g_pre, g_post, w_in, b_f, w_out):
    D = h.shape[1]
    n_heads = b_f.shape[0]
    n_main = 4 * D
    w_cat = jnp.concatenate(
        [w_in, jnp.zeros((D, LANES - n_heads), w_in.dtype)], axis=1).astype(BF16)
    p, tail = _norm_proj(h, g_pre.reshape(1, D), w_cat, q_cols=D, q_scale=LOG2E * HEAD ** -0.5)
    f_t = tail[:, :n_heads].reshape(B, L, n_heads).transpose(0, 2, 1).reshape(B * n_heads, L)
    b_col = jnp.tile(b_f, B).reshape(B * n_heads, 1)
    cum = _fox_cum(f_t, b_col)
    cum5 = cum.reshape(B, n_heads // 2, 2, L // BLK, BLK).transpose(0, 1, 3, 2, 4)
    z = _fox_attn(p.reshape(B, L, n_main), cum5)
    return _out_proj(z.reshape(B * L, D), w_out.astype(BF16), h, g_post.reshape(1, D))


def _rwkv_layer(h, B, L, g_pre, g_post, w_in, mu, w0, w_up, a0, a_up, k_k, k_a, r_k,
                ln_w, ln_b, w_out):
    D = h.shape[1]
    n_main = 4 * D
    p, tail = _norm_proj(h, g_pre.reshape(1, D), w_in.astype(BF16))
    row = lambda t: t.reshape(1, -1)
    z = _wkv(p.reshape(B, L, n_main), tail.reshape(B, L, LANES),
             row(mu[:n_main]), row(mu[n_main:]), row(w0), w_up.astype(BF16), row(a0),
             a_up.astype(BF16), row(k_k), row(k_a), row(r_k), row(ln_w), row(ln_b))
    return _out_proj(z.reshape(B * L, D), w_out.astype(BF16), h, g_post.reshape(1, D))


def kernel(x, meta_tokens, norm_pre, norm_post, fox_w_in, fox_b_f, fox_w_out, rwkv_w_in, rwkv_mu, rwkv_w0, rwkv_w_up, rwkv_a0, rwkv_a_up, rwkv_k_k, rwkv_k_a, rwkv_r_k, rwkv_ln_w, rwkv_ln_b, rwkv_w_out):
    B, S, D = x.shape
    depth = norm_pre.shape[0]
    L0 = N_META + S
    L = -(-L0 // BLK) * BLK
    assert (B * L) % ROW_TILE == 0 and D % LANES == 0
    meta = jnp.broadcast_to(meta_tokens[None].astype(x.dtype), (B, N_META, D))
    h = jnp.concatenate([meta, x, jnp.zeros((B, L - L0, D), x.dtype)], axis=1).reshape(B * L, D)
    for i in range(depth):
        j = i // 2
        if i % 2 == 0:
            h = _fox_layer(h, B, L, norm_pre[i], norm_post[i], fox_w_in[j], fox_b_f[j], fox_w_out[j])
        else:
            h = _rwkv_layer(h, B, L, norm_pre[i], norm_post[i], rwkv_w_in[j], rwkv_mu[j],
                            rwkv_w0[j], rwkv_w_up[j], rwkv_a0[j], rwkv_a_up[j], rwkv_k_k[j],
                            rwkv_k_a[j], rwkv_r_k[j], rwkv_ln_w[j], rwkv_ln_b[j], rwkv_w_out[j])
    return h.reshape(B, L, D)[:, N_META:L0]
```

```python
import functools
import math

import jax
import jax.numpy as jnp
from jax import lax
from jax.experimental import pallas as pl
from jax.experimental.pallas import tpu as pltpu

F32 = jnp.float32
BF16 = jnp.bfloat16

N_META = 16
HEAD = 64
LANES = 128
BLK = 128
ATTN_PAIRS = 8
ATTN_SUBGROUP = 8
KV_GROUP = 4
CHUNK = 64
WKV_BATCH = 2
ROW_TILE = 512
COL_TILE = 512
NORM_EPS = 1e-6
GN_EPS = 64e-5
DECAY_SCALE = math.exp(-0.5)
LOG2E = math.log2(math.e)
NEG = -1e30
VMEM_LIMIT = 48 * 1024 * 1024


def _dot(a, b):
    return jnp.dot(a, b, preferred_element_type=F32)


def _dot_nt(a, b):
    return lax.dot_general(a, b, (((1,), (1,)), ((), ())), preferred_element_type=F32)


def _dot_tn(a, b):
    return lax.dot_general(a, b, (((0,), (0,)), ((), ())), preferred_element_type=F32)


def _split3(x):
    hi = x.astype(BF16)
    r1 = x - hi.astype(F32)
    mid = r1.astype(BF16)
    lo = (r1 - mid.astype(F32)).astype(BF16)
    return hi, mid, lo


def _sigmoid(x):
    return 0.5 * jnp.tanh(0.5 * x) + 0.5


def _norm_proj_kernel(h_ref, g_ref, w_ref, o_ref, t_ref, *, n_main, q_cols, q_scale):
    x = h_ref[...]
    var = jnp.mean(x * x, axis=-1, keepdims=True)
    u = (x * lax.rsqrt(var + NORM_EPS) * g_ref[...]).astype(BF16)
    for j in range(0, n_main, COL_TILE):
        acc = _dot(u, w_ref[:, j:j + COL_TILE])
        if j < q_cols:
            acc = acc * q_scale
        o_ref[:, j:j + COL_TILE] = acc.astype(BF16)
    t_ref[...] = _dot(u, w_ref[:, n_main:])


def _norm_proj(h, g, w, *, q_cols=0, q_scale=1.0):
    T, D = h.shape
    n_all = w.shape[1]
    n_main = n_all - LANES
    kern = functools.partial(_norm_proj_kernel, n_main=n_main, q_cols=q_cols, q_scale=q_scale)
    return pl.pallas_call(
        kern,
        grid=(T // ROW_TILE,),
        in_specs=[
            pl.BlockSpec((ROW_TILE, D), lambda i: (i, 0)),
            pl.BlockSpec((1, D), lambda i: (0, 0)),
            pl.BlockSpec((D, n_all), lambda i: (0, 0)),
        ],
        out_specs=[
            pl.BlockSpec((ROW_TILE, n_main), lambda i: (i, 0)),
            pl.BlockSpec((ROW_TILE, LANES), lambda i: (i, 0)),
        ],
        out_shape=[
            jax.ShapeDtypeStruct((T, n_main), BF16),
            jax.ShapeDtypeStruct((T, LANES), F32),
        ],
        compiler_params=pltpu.CompilerParams(
            dimension_semantics=("arbitrary",), vmem_limit_bytes=VMEM_LIMIT),
        name="norm_proj",
    )(h, g, w)


def _out_proj_kernel(z_ref, w_ref, h_ref, g_ref, o_ref):
    m = _dot(z_ref[...], w_ref[...])
    var = jnp.mean(m * m, axis=-1, keepdims=True)
    o_ref[...] = h_ref[...] + m * lax.rsqrt(var + NORM_EPS) * g_ref[...]


def _out_proj(z, w, h, g):
    T, D = h.shape
    return pl.pallas_call(
        _out_proj_kernel,
        grid=(T // ROW_TILE,),
        in_specs=[
            pl.BlockSpec((ROW_TILE, D), lambda i: (i, 0)),
            pl.BlockSpec((D, D), lambda i: (0, 0)),
            pl.BlockSpec((ROW_TILE, D), lambda i: (i, 0)),
            pl.BlockSpec((1, D), lambda i: (0, 0)),
        ],
        out_specs=pl.BlockSpec((ROW_TILE, D), lambda i: (i, 0)),
        out_shape=jax.ShapeDtypeStruct((T, D), F32),
        compiler_params=pltpu.CompilerParams(
            dimension_semantics=("arbitrary",), vmem_limit_bytes=VMEM_LIMIT),
        name="out_proj",
    )(z, w, h, g)


def _cum_kernel(x_ref, b_ref, o_ref, carry_ref):
    j = pl.program_id(0)

    @pl.when(j == 0)
    def _():
        carry_ref[...] = jnp.zeros_like(carry_ref)

    x = x_ref[...] + b_ref[...]
    lf = jnp.minimum(x, 0.0) - jnp.log(1.0 + jnp.exp(-jnp.abs(x)))
    row = lax.broadcasted_iota(jnp.int32, (BLK, BLK), 0)
    col = lax.broadcasted_iota(jnp.int32, (BLK, BLK), 1)
    tri = (row <= col).astype(BF16)
    hi, mid, lo = _split3(lf)
    out = _dot(hi, tri) + _dot(mid, tri) + _dot(lo, tri) + carry_ref[...]
    o_ref[...] = out
    carry_ref[...] = out[:, BLK - 1:BLK]


def _fox_cum(f_t, b_col):
    R, L = f_t.shape
    return pl.pallas_call(
        _cum_kernel,
        grid=(L // BLK,),
        in_specs=[
            pl.BlockSpec((R, BLK), lambda j: (0, j)),
            pl.BlockSpec((R, 1), lambda j: (0, 0)),
        ],
        out_specs=pl.BlockSpec((R, BLK), lambda j: (0, j)),
        out_shape=jax.ShapeDtypeStruct((R, L), F32),
        scratch_shapes=[pltpu.VMEM((R, 1), F32)],
        compiler_params=pltpu.CompilerParams(dimension_semantics=("arbitrary",)),
        name="fox_cum",
    )(f_t, b_col)


def _fox_ext_kernel(c_ref, o_ref):
    nc = c_ref[0] * (-LOG2E)
    width = o_ref.shape[2]
    head = lax.broadcasted_iota(jnp.int32, (LANES, width), 0)
    col = lax.broadcasted_iota(jnp.int32, (LANES, width), 1)
    base = (head // 2) * LANES + (head % 2) * 3
    valid = head < (width // LANES) * 2
    ext = jnp.zeros((c_ref.shape[1], width), F32)
    for o, term in enumerate(_split3(nc)):
        ext = ext + _dot(term, (valid & (col == base + o)).astype(BF16))
    lane = lax.broadcasted_iota(jnp.int32, (1, width), 1) % LANES
    o_ref[0] = jnp.where((lane >= 6) & (lane < 9), 1.0, ext).astype(BF16)


def _fox_ext(cum_col, width):
    B, L, _ = cum_col.shape
    return pl.pallas_call(
        _fox_ext_kernel,
        grid=(B, L // BLK),
        in_specs=[pl.BlockSpec((1, BLK, LANES), lambda b, j: (b, j, 0))],
        out_specs=pl.BlockSpec((1, BLK, width), lambda b, j: (b, j, 0)),
        out_shape=jax.ShapeDtypeStruct((B, L, width), BF16),
        compiler_params=pltpu.CompilerParams(dimension_semantics=("arbitrary", "arbitrary")),
        name="fox_ext",
    )(cum_col)


def _fox_attn_kernel(q_ref, k_ref, x_ref, vt_ref, g_ref, c_ref, o_ref):
    i = pl.program_id(2)
    n_p = q_ref.shape[2] // LANES
    pairs = range(n_p)
    ps = [slice(p * LANES, (p + 1) * LANES) for p in pairs]
    lane = lax.broadcasted_iota(jnp.int32, (1, LANES), 1)
    is_a = lane < HEAD
    zq = jnp.zeros((), BF16)

    def ext_row(cref, h):
        hi, mid, lo = (t.astype(F32) for t in _split3(cref))
        row = jnp.where((lane >= 3 * h) & (lane < 3 * h + 3), 1.0, 0.0)
        row = row + jnp.where(lane == 6, hi, 0.0) + jnp.where(lane == 7, mid, 0.0) \
            + jnp.where(lane == 8, lo, 0.0)
        return jnp.broadcast_to(row, (BLK, LANES)).astype(BF16)

    qx = []
    for p in pairs:
        q2 = q_ref[0, :, ps[p]]
        cref = c_ref[0, p, i][:, 0:1] * LOG2E
        qx.append(jnp.concatenate(
            [jnp.concatenate([jnp.where(is_a, q2, zq), ext_row(cref[0:1], 0)], axis=1),
             jnp.concatenate([jnp.where(is_a, zq, q2), ext_row(cref[1:2], 1)], axis=1)], axis=0))

    key_i = lax.broadcasted_iota(jnp.int32, (BLK, 2 * BLK), 0)
    qry_i = lax.broadcasted_iota(jnp.int32, (BLK, 2 * BLK), 1) % BLK
    keep = key_i <= qry_i

    def step(j, carry, nb=1):
        m, l, acc = carry
        off = pl.multiple_of(j * BLK, BLK)
        unmasked = jnp.logical_or(keep, j + (nb - 1) != i)
        m, l, acc = list(m), list(l), list(acc)
        n_sub = ATTN_SUBGROUP if nb > 1 else n_p
        for g0 in range(0, n_p, n_sub):
            sub = range(g0, min(g0 + n_sub, n_p))
            st = {}
            for p in sub:
                kx = jnp.concatenate([k_ref[0, pl.ds(off, nb * BLK), ps[p]],
                                      x_ref[0, pl.ds(off, nb * BLK), ps[p]]], axis=1)
                s = _dot_nt(kx, qx[p])
                last = jnp.where(unmasked, s[(nb - 1) * BLK:], NEG)
                st[p] = last if nb == 1 else jnp.concatenate([s[:(nb - 1) * BLK], last], axis=0)
            for p in sub:
                mn = jnp.maximum(m[p], jnp.max(st[p], axis=0, keepdims=True))
                alpha = jnp.exp2(m[p] - mn)
                pr = jnp.exp2(st[p] - mn)
                l[p] = l[p] * alpha + jnp.sum(pr, axis=0, keepdims=True)
                vt = jnp.concatenate([vt_ref[0, j + t, ps[p], :] for t in range(nb)], axis=1)
                acc[p] = acc[p] * alpha + _dot(vt, pr.astype(BF16))
                m[p] = mn
        return tuple(m), tuple(l), tuple(acc)

    n_grp = (i + 1) // KV_GROUP
    carry = (tuple(jnp.full((1, 2 * BLK), NEG, F32) for _ in pairs),
             tuple(jnp.zeros((1, 2 * BLK), F32) for _ in pairs),
             tuple(jnp.zeros((LANES, 2 * BLK), F32) for _ in pairs))
    carry = lax.fori_loop(0, n_grp, lambda jj, c: step(jj * KV_GROUP, c, KV_GROUP), carry)
    _, l, acc = lax.fori_loop(n_grp * KV_GROUP, i + 1, step, carry)
    for p in pairs:
        o_t = acc[p] / l[p]
        o = jnp.concatenate([o_t[:HEAD, :BLK], o_t[HEAD:, BLK:]], axis=0).T
        g = g_ref[0, :, ps[p]].astype(F32)
        o_ref[0, :, ps[p]] = (o * (g * _sigmoid(g))).astype(BF16)


def _fox_attn(p3, kext, v_t, cum5):
    B, L, _ = p3.shape
    n_pair = (p3.shape[2] // 4) // LANES
    nq = L // BLK
    n_p = ATTN_PAIRS
    width = n_p * LANES
    n_grp = n_pair // n_p
    return pl.pallas_call(
        _fox_attn_kernel,
        grid=(B, n_grp, nq),
        in_specs=[
            pl.BlockSpec((1, BLK, width), lambda b, p, i: (b, i, p)),
            pl.BlockSpec((1, L, width), lambda b, p, i: (b, 0, n_grp + p)),
            pl.BlockSpec((1, L, width), lambda b, p, i: (b, 0, p)),
            pl.BlockSpec((1, nq, width, BLK), lambda b, p, i: (b, 0, p, 0)),
            pl.BlockSpec((1, BLK, width), lambda b, p, i: (b, i, 3 * n_grp + p)),
            pl.BlockSpec((1, n_p, nq, 2, BLK), lambda b, p, i: (b, p, 0, 0, 0)),
        ],
        out_specs=pl.BlockSpec((1, BLK, width), lambda b, p, i: (b, i, p)),
        out_shape=jax.ShapeDtypeStruct((B, L, n_pair * LANES), BF16),
        compiler_params=pltpu.CompilerParams(
            dimension_semantics=("arbitrary", "arbitrary", "arbitrary"), vmem_limit_bytes=VMEM_LIMIT),
        name="fox_attn",
    )(p3, p3, kext, v_t, p3, cum5)


def _head_sums(x):
    row = lax.broadcasted_iota(jnp.int32, (LANES, LANES), 0)
    col = lax.broadcasted_iota(jnp.int32, (LANES, LANES), 1)
    ones = ((row // HEAD) == (col // HEAD)).astype(BF16)
    parts = []
    for g in range(x.shape[1] // LANES):
        parts.append(_dot(x[:, g * LANES:(g + 1) * LANES].astype(BF16), ones))
    return jnp.concatenate(parts, axis=1)


def _wkv_kernel(r_ref, k_ref, v_ref, g_ref, rh_ref, kh_ref, vh_ref, gh_ref, t_ref, th_ref,
                mu_ref, mut_ref, w0_ref, wup_ref, a0_ref, aup_ref, kk_ref, ka_ref, rk_ref,
                lnw_ref, lnb_ref, o_ref, st_ref):
    c = pl.program_id(1)

    @pl.when(c == 0)
    def _():
        st_ref[...] = jnp.zeros_like(st_ref)

    n_b = r_ref.shape[0]
    D = r_ref.shape[2]
    n_pair = D // LANES
    not_first = c != 0
    row = lax.broadcasted_iota(jnp.int32, (CHUNK, CHUNK), 0)
    col = lax.broadcasted_iota(jnp.int32, (CHUNK, CHUNK), 1)
    sub_diag = (col + 1 == row).astype(BF16)
    tril = (col <= row).astype(BF16)
    top0 = (lax.broadcasted_iota(jnp.int32, (8, 1), 0) == 0) & not_first
    trow0 = lax.broadcasted_iota(jnp.int32, (CHUNK, 1), 0) == 0

    def lerp(cur, sh, prev, mu):
        sh = jnp.concatenate([sh[0:8] + jnp.where(top0, prev, 0.0), sh[8:]], axis=0)
        return cur + (sh - cur) * mu

    def shifted(cur, prev, mu):
        return lerp(cur.astype(F32), _dot(sub_diag, cur), prev.astype(F32), mu)

    def prep(b):
        r = shifted(r_ref[b], rh_ref[b, 7:8, :], mu_ref[:, 0:D])
        k = shifted(k_ref[b], kh_ref[b, 7:8, :], mu_ref[:, D:2 * D])
        v = shifted(v_ref[b], vh_ref[b, 7:8, :], mu_ref[:, 2 * D:3 * D])
        g = shifted(g_ref[b], gh_ref[b, 7:8, :], mu_ref[:, 3 * D:4 * D])
        tcur = t_ref[b]
        tl = lerp(tcur, jnp.where(trow0, 0.0, pltpu.roll(tcur, 1, axis=0)), th_ref[b, 7:8, :],
                  mut_ref[...])
        w_log = w0_ref[...] + _dot(jnp.tanh(tl[:, :HEAD]).astype(BF16), wup_ref[...])
        lw = (-0.5 * DECAY_SCALE * LOG2E) * jnp.tanh(0.5 * w_log) - 0.5 * DECAY_SCALE * LOG2E
        a = _sigmoid(a0_ref[...] + _dot(tl[:, HEAD:].astype(BF16), aup_ref[...]))
        kk = k * kk_ref[...]
        kk = kk * lax.rsqrt(jnp.maximum(_head_sums(kk * kk), 1e-24))
        k = k * (1.0 + (a - 1.0) * ka_ref[...])
        hi, mid, lo = _split3(lw)
        cl = _dot(tril, hi) + _dot(tril, mid) + _dot(tril, lo)
        e_neg = jnp.exp2(-cl)
        return dict(
            r=r, k=k, v=v, g=g,
            r_t=(r * jnp.exp2(cl)).astype(BF16),
            a_t=(-kk * jnp.exp2(cl - lw)).astype(BF16),
            b_t=(kk * a * e_neg).astype(BF16),
            k_t=(k * e_neg).astype(BF16),
            w_end=jnp.exp2(cl[CHUNK - 1:CHUNK, :]),
            v_b=v.astype(BF16))

    pre = [prep(b) for b in range(n_b)]

    lane = lax.broadcasted_iota(jnp.int32, (1, LANES), 1)
    is_a = lane < HEAD
    trw = lax.broadcasted_iota(jnp.int32, (CHUNK, LANES), 0)
    jw = lax.broadcasted_iota(jnp.int32, (CHUNK, LANES), 1) % HEAD
    strict = jw < trw
    incl = jw <= trw
    r2 = lax.broadcasted_iota(jnp.int32, (LANES, LANES), 0) // HEAD
    c2 = lax.broadcasted_iota(jnp.int32, (LANES, LANES), 1) // HEAD
    same_head = r2 == c2
    zb16 = jnp.zeros((), BF16)

    def bdiag(xw, swap=False):
        za = jnp.where(is_a, xw, jnp.zeros_like(xw))
        zb = jnp.where(is_a, jnp.zeros_like(xw), xw)
        return jnp.concatenate([zb, za] if swap else [za, zb], axis=0)

    streams = [(b, p) for b in range(n_b) for p in range(n_pair)]
    ns = range(len(streams))

    def part(name, n):
        b, p = streams[n]
        return pre[b][name][:, p * LANES:(p + 1) * LANES]

    xq = [jnp.concatenate([part("a_t", n), part("r_t", n)], axis=0) for n in ns]
    ycat = [jnp.concatenate(
        [jnp.where(is_a, jnp.concatenate([part("k_t", n), part("b_t", n)], axis=0), zb16),
         jnp.where(is_a, zb16, jnp.concatenate([part("b_t", n), part("k_t", n)], axis=0))],
        axis=0) for n in ns]
    sc = [_dot_nt(xq[n], ycat[n]) for n in ns]
    s0 = [st_ref[b, p] for b, p in streams]
    xs = [_dot_nt(xq[n], s0[n].astype(BF16)) for n in ns]
    top_lo = [sc[n][:CHUNK, :LANES] for n in ns]
    top_hi = [sc[n][:CHUNK, LANES:] for n in ns]
    bot_lo = [sc[n][CHUNK:, :LANES] for n in ns]
    bot_hi = [sc[n][CHUNK:, LANES:] for n in ns]
    a_ak = [jnp.where(strict, jnp.where(is_a, top_lo[n], top_hi[n]), 0.0).astype(BF16) for n in ns]
    a_ab = [jnp.where(strict, jnp.where(is_a, top_hi[n], top_lo[n]), 0.0) for n in ns]
    m_rk = [jnp.where(incl, jnp.where(is_a, bot_lo[n], bot_hi[n]), 0.0).astype(BF16) for n in ns]
    m_rb = [jnp.where(incl, jnp.where(is_a, bot_hi[n], bot_lo[n]), 0.0).astype(BF16) for n in ns]

    eye = (jw == trw).astype(F32)
    lvl1 = (trw // 2 == jw // 2) & strict
    tw = [eye + jnp.where(lvl1, a_ab[n], 0.0) for n in ns]
    s = 2
    while s < CHUNK:
        off = (trw // (2 * s) == jw // (2 * s)) & ((trw // s) % 2 == 1) & ((jw // s) % 2 == 0)
        tb = [t.astype(BF16) for t in tw]
        pm = [_dot(tb[n], bdiag(jnp.where(off, a_ab[n], 0.0).astype(BF16))) for n in ns]
        tw = [tw[n] + _dot(pm[n].astype(BF16), bdiag(tb[n])) for n in ns]
        s *= 2

    vbd = [bdiag(part("v_b", n)) for n in ns]
    gm = [xs[n][:CHUNK] + _dot(a_ak[n], vbd[n]) for n in ns]
    u = [_dot(tw[n].astype(BF16), bdiag(gm[n].astype(BF16), swap=True)).astype(BF16) for n in ns]
    ys = [xs[n][CHUNK:] + _dot(jnp.concatenate([m_rb[n], m_rk[n]], axis=1),
                               jnp.concatenate([bdiag(u[n], swap=True), vbd[n]], axis=0))
          for n in ns]
    for n in ns:
        b, p = streams[n]
        uv = jnp.concatenate([u[n], part("v_b", n)], axis=0)
        bk = jnp.concatenate([part("b_t", n), part("k_t", n)], axis=0)
        st_ref[b, p] = (s0[n] + jnp.where(same_head, _dot_tn(uv, bk), 0.0)) * part("w_end", n)

    inv_n = 1.0 / HEAD
    for b in range(n_b):
        y = jnp.concatenate(ys[b * n_pair:(b + 1) * n_pair], axis=1)
        q = pre[b]
        mean = _head_sums(y) * inv_n
        d = y - mean
        var = _head_sums(d * d) * inv_n
        yn = d * lax.rsqrt(var + GN_EPS) * lnw_ref[...] + lnb_ref[...]
        bonus = _head_sums(q["r"] * q["k"] * rk_ref[...]) * q["v"]
        o_ref[b] = ((yn + bonus) * (q["g"] * _sigmoid(q["g"]))).astype(BF16)


def _wkv(p3, tail3, mu, mu_t, w0, w_up, a0, a_up, k_k, k_a, r_k, ln_w, ln_b):
    B, L, n_main = p3.shape
    D = n_main // 4
    nc = L // CHUNK
    sub = CHUNK // 8
    n_b = WKV_BATCH if B % WKV_BATCH == 0 else 1

    def cur(s):
        return pl.BlockSpec((n_b, CHUNK, D), lambda b, c: (b, c, s))

    def halo(s):
        return pl.BlockSpec((n_b, 8, D), lambda b, c: (b, jnp.maximum(c * sub - 1, 0), s))

    def vec(n):
        return pl.BlockSpec((1, n), lambda b, c: (0, 0))

    lora = pl.BlockSpec((HEAD, D), lambda b, c: (0, 0))
    return pl.pallas_call(
        _wkv_kernel,
        grid=(B // n_b, nc),
        in_specs=[cur(0), cur(1), cur(2), cur(3), halo(0), halo(1), halo(2), halo(3),
                  pl.BlockSpec((n_b, CHUNK, LANES), lambda b, c: (b, c, 0)),
                  pl.BlockSpec((n_b, 8, LANES), lambda b, c: (b, jnp.maximum(c * sub - 1, 0), 0)),
                  vec(n_main), vec(LANES), vec(D), lora, vec(D), lora,
                  vec(D), vec(D), vec(D), vec(D), vec(D)],
        out_specs=pl.BlockSpec((n_b, CHUNK, D), lambda b, c: (b, c, 0)),
        out_shape=jax.ShapeDtypeStruct((B, L, D), BF16),
        scratch_shapes=[pltpu.VMEM((n_b, D // LANES, LANES, LANES), F32)],
        compiler_params=pltpu.CompilerParams(
            dimension_semantics=("arbitrary", "arbitrary"), vmem_limit_bytes=VMEM_LIMIT),
        name="wkv7",
    )(p3, p3, p3, p3, p3, p3, p3, p3, tail3, tail3,
      mu, mu_t, w0, w_up, a0, a_up, k_k, k_a, r_k, ln_w, ln_b)


def _fox_layer(h, B, L, g_pre, g_post, w_in, b_f, w_out):
    D = h.shape[1]
    n_heads = b_f.shape[0]
    n_main = 4 * D
    w_cat = jnp.concatenate(
        [w_in, jnp.zeros((D, LANES - n_heads), w_in.dtype)], axis=1).astype(BF16)
    p, tail = _norm_proj(h, g_pre.reshape(1, D), w_cat, q_cols=D, q_scale=LOG2E * HEAD ** -0.5)
    f_t = tail[:, :n_heads].reshape(B, L, n_heads).transpose(0, 2, 1).reshape(B * n_heads, L)
    b_col = jnp.tile(b_f, B).reshape(B * n_heads, 1)
    cum = _fox_cum(f_t, b_col)
    cum3 = cum.reshape(B, n_heads, L)
    cum5 = cum3.reshape(B, n_heads // 2, 2, L // BLK, BLK).transpose(0, 1, 3, 2, 4)
    cum_col = jnp.pad(cum3.transpose(0, 2, 1), ((0, 0), (0, 0), (0, LANES - n_heads)))
    p3 = p.reshape(B, L, n_main)
    v_t = p3[:, :, 2 * D:3 * D].reshape(B, L // BLK, BLK, D).transpose(0, 1, 3, 2)
    z = _fox_attn(p3, _fox_ext(cum_col, D), v_t, cum5)
    return _out_proj(z.reshape(B * L, D), w_out.astype(BF16), h, g_post.reshape(1, D))


def _rwkv_layer(h, B, L, g_pre, g_post, w_in, mu, w0, w_up, a0, a_up, k_k, k_a, r_k,
                ln_w, ln_b, w_out):
    D = h.shape[1]
    n_main = 4 * D
    p, tail = _norm_proj(h, g_pre.reshape(1, D), w_in.astype(BF16))
    row = lambda t: t.reshape(1, -1)
    z = _wkv(p.reshape(B, L, n_main), tail.reshape(B, L, LANES),
             row(mu[:n_main]), row(mu[n_main:]), row(w0), w_up.astype(BF16), row(a0),
             a_up.astype(BF16), row(k_k), row(k_a), row(r_k), row(ln_w), row(ln_b))
    return _out_proj(z.reshape(B * L, D), w_out.astype(BF16), h, g_post.reshape(1, D))


def kernel(x, meta_tokens, norm_pre, norm_post, fox_w_in, fox_b_f, fox_w_out, rwkv_w_in, rwkv_mu, rwkv_w0, rwkv_w_up, rwkv_a0, rwkv_a_up, rwkv_k_k, rwkv_k_a, rwkv_r_k, rwkv_ln_w, rwkv_ln_b, rwkv_w_out):
    B, S, D = x.shape
    depth = norm_pre.shape[0]
    L0 = N_META + S
    L = -(-L0 // BLK) * BLK
    assert (B * L) % ROW_TILE == 0 and D % LANES == 0
    meta = jnp.broadcast_to(meta_tokens[None].astype(x.dtype), (B, N_META, D))
    h = jnp.concatenate([meta, x, jnp.zeros((B, L - L0, D), x.dtype)], axis=1).reshape(B * L, D)
    for i in range(depth):
        j = i // 2
        if i % 2 == 0:
            h = _fox_layer(h, B, L, norm_pre[i], norm_post[i], fox_w_in[j], fox_b_f[j], fox_w_out[j])
        else:
            h = _rwkv_layer(h, B, L, norm_pre[i], norm_post[i], rwkv_w_in[j], rwkv_mu[j],
                            rwkv_w0[j], rwkv_w_up[j], rwkv_a0[j], rwkv_a_up[j], rwkv_k_k[j],
                            rwkv_k_a[j], rwkv_r_k[j], rwkv_ln_w[j], rwkv_ln_b[j], rwkv_w_out[j])
    return h.reshape(B, L, D)[:, N_META:L0]
```

```python
import functools
import math

import jax
import jax.numpy as jnp
from jax import lax
from jax.experimental import pallas as pl
from jax.experimental.pallas import tpu as pltpu

F32 = jnp.float32
BF16 = jnp.bfloat16

N_META = 16
HEAD = 64
LANES = 128
BLK = 128
ATTN_PAIRS = 8
ATTN_SUBGROUP = 8
KV_GROUP = 4
CHUNK = 64
WKV_BATCH = 4
ROW_TILE = 512
COL_TILE = 512
NORM_EPS = 1e-6
GN_EPS = 64e-5
DECAY_SCALE = math.exp(-0.5)
LOG2E = math.log2(math.e)
NEG = -1e30
VMEM_LIMIT = 48 * 1024 * 1024
VMEM_LIMIT_FUSED = 56 * 1024 * 1024


def _dot(a, b):
    return jnp.dot(a, b, preferred_element_type=F32)


def _dot_nt(a, b):
    return lax.dot_general(a, b, (((1,), (1,)), ((), ())), preferred_element_type=F32)


def _dot_tn(a, b):
    return lax.dot_general(a, b, (((0,), (0,)), ((), ())), preferred_element_type=F32)


def _split3(x):
    hi = x.astype(BF16)
    r1 = x - hi.astype(F32)
    mid = r1.astype(BF16)
    lo = (r1 - mid.astype(F32)).astype(BF16)
    return hi, mid, lo


def _sigmoid(x):
    return 0.5 * jnp.tanh(0.5 * x) + 0.5


def _project(x, g_ref, w_ref, o_ref, t_ref, n_main, q_cols, q_scale):
    var = jnp.mean(x * x, axis=-1, keepdims=True)
    u = (x * lax.rsqrt(var + NORM_EPS) * g_ref[...]).astype(BF16)
    for j in range(0, n_main, COL_TILE):
        acc = _dot(u, w_ref[:, j:j + COL_TILE])
        if j < q_cols:
            acc = acc * q_scale
        o_ref[:, j:j + COL_TILE] = acc.astype(BF16)
    t_ref[...] = _dot(u, w_ref[:, n_main:])


def _norm_proj_kernel(h_ref, g_ref, w_ref, o_ref, t_ref, *, n_main, q_cols, q_scale):
    _project(h_ref[...], g_ref, w_ref, o_ref, t_ref, n_main, q_cols, q_scale)


def _norm_proj(h, g, w, *, q_cols=0, q_scale=1.0):
    T, D = h.shape
    n_all = w.shape[1]
    n_main = n_all - LANES
    kern = functools.partial(_norm_proj_kernel, n_main=n_main, q_cols=q_cols, q_scale=q_scale)
    return pl.pallas_call(
        kern,
        grid=(T // ROW_TILE,),
        in_specs=[
            pl.BlockSpec((ROW_TILE, D), lambda i: (i, 0)),
            pl.BlockSpec((1, D), lambda i: (0, 0)),
            pl.BlockSpec((D, n_all), lambda i: (0, 0)),
        ],
        out_specs=[
            pl.BlockSpec((ROW_TILE, n_main), lambda i: (i, 0)),
            pl.BlockSpec((ROW_TILE, LANES), lambda i: (i, 0)),
        ],
        out_shape=[
            jax.ShapeDtypeStruct((T, n_main), BF16),
            jax.ShapeDtypeStruct((T, LANES), F32),
        ],
        compiler_params=pltpu.CompilerParams(
            dimension_semantics=("arbitrary",), vmem_limit_bytes=VMEM_LIMIT),
        name="norm_proj",
    )(h, g, w)


def _residual(z_ref, w_ref, h_ref, g_ref):
    m = _dot(z_ref[...], w_ref[...])
    var = jnp.mean(m * m, axis=-1, keepdims=True)
    return h_ref[...] + m * lax.rsqrt(var + NORM_EPS) * g_ref[...]


def _out_proj_kernel(z_ref, w_ref, h_ref, g_ref, o_ref):
    o_ref[...] = _residual(z_ref, w_ref, h_ref, g_ref)


def _out_norm_proj_kernel(z_ref, wo_ref, h_ref, gpost_ref, gpre_ref, wi_ref, ho_ref, o_ref, t_ref,
                          *, n_main, q_cols, q_scale):
    hn = _residual(z_ref, wo_ref, h_ref, gpost_ref)
    ho_ref[...] = hn
    _project(hn, gpre_ref, wi_ref, o_ref, t_ref, n_main, q_cols, q_scale)


def _out_norm_proj(z, w_out, h, g_post, g_pre, w_in, *, q_cols=0, q_scale=1.0):
    T, D = h.shape
    n_all = w_in.shape[1]
    n_main = n_all - LANES
    kern = functools.partial(_out_norm_proj_kernel, n_main=n_main, q_cols=q_cols, q_scale=q_scale)
    rows = lambda n: pl.BlockSpec((ROW_TILE, n), lambda i: (i, 0))
    whole = lambda a: pl.BlockSpec(a.shape, lambda i: (0, 0))
    return pl.pallas_call(
        kern,
        grid=(T // ROW_TILE,),
        in_specs=[rows(D), whole(w_out), rows(D), whole(g_post), whole(g_pre), whole(w_in)],
        out_specs=[rows(D), rows(n_main), rows(LANES)],
        out_shape=[
            jax.ShapeDtypeStruct((T, D), F32),
            jax.ShapeDtypeStruct((T, n_main), BF16),
            jax.ShapeDtypeStruct((T, LANES), F32),
        ],
        compiler_params=pltpu.CompilerParams(
            dimension_semantics=("arbitrary",), vmem_limit_bytes=VMEM_LIMIT_FUSED),
        name="out_norm_proj",
    )(z, w_out, h, g_post, g_pre, w_in)


def _out_proj(z, w, h, g):
    T, D = h.shape
    return pl.pallas_call(
        _out_proj_kernel,
        grid=(T // ROW_TILE,),
        in_specs=[
            pl.BlockSpec((ROW_TILE, D), lambda i: (i, 0)),
            pl.BlockSpec((D, D), lambda i: (0, 0)),
            pl.BlockSpec((ROW_TILE, D), lambda i: (i, 0)),
            pl.BlockSpec((1, D), lambda i: (0, 0)),
        ],
        out_specs=pl.BlockSpec((ROW_TILE, D), lambda i: (i, 0)),
        out_shape=jax.ShapeDtypeStruct((T, D), F32),
        compiler_params=pltpu.CompilerParams(
            dimension_semantics=("arbitrary",), vmem_limit_bytes=VMEM_LIMIT),
        name="out_proj",
    )(z, w, h, g)


def _cum_kernel(x_ref, b_ref, o_ref, carry_ref):
    j = pl.program_id(0)

    @pl.when(j == 0)
    def _():
        carry_ref[...] = jnp.zeros_like(carry_ref)

    x = x_ref[...] + b_ref[...]
    lf = jnp.minimum(x, 0.0) - jnp.log(1.0 + jnp.exp(-jnp.abs(x)))
    row = lax.broadcasted_iota(jnp.int32, (BLK, BLK), 0)
    col = lax.broadcasted_iota(jnp.int32, (BLK, BLK), 1)
    tri = (row <= col).astype(BF16)
    hi, mid, lo = _split3(lf)
    out = _dot(hi, tri) + _dot(mid, tri) + _dot(lo, tri) + carry_ref[...]
    o_ref[...] = out
    carry_ref[...] = out[:, BLK - 1:BLK]


def _fox_cum(f_t, b_col):
    R, L = f_t.shape
    return pl.pallas_call(
        _cum_kernel,
        grid=(L // BLK,),
        in_specs=[
            pl.BlockSpec((R, BLK), lambda j: (0, j)),
            pl.BlockSpec((R, 1), lambda j: (0, 0)),
        ],
        out_specs=pl.BlockSpec((R, BLK), lambda j: (0, j)),
        out_shape=jax.ShapeDtypeStruct((R, L), F32),
        scratch_shapes=[pltpu.VMEM((R, 1), F32)],
        compiler_params=pltpu.CompilerParams(dimension_semantics=("arbitrary",)),
        name="fox_cum",
    )(f_t, b_col)


def _fox_ext_kernel(c_ref, o_ref):
    nc = c_ref[0] * (-LOG2E)
    width = o_ref.shape[2]
    head = lax.broadcasted_iota(jnp.int32, (LANES, width), 0)
    col = lax.broadcasted_iota(jnp.int32, (LANES, width), 1)
    base = (head // 2) * LANES + (head % 2) * 3
    valid = head < (width // LANES) * 2
    ext = jnp.zeros((c_ref.shape[1], width), F32)
    for o, term in enumerate(_split3(nc)):
        ext = ext + _dot(term, (valid & (col == base + o)).astype(BF16))
    lane = lax.broadcasted_iota(jnp.int32, (1, width), 1) % LANES
    o_ref[0] = jnp.where((lane >= 6) & (lane < 9), 1.0, ext).astype(BF16)


def _fox_ext(cum_col, width):
    B, L, _ = cum_col.shape
    return pl.pallas_call(
        _fox_ext_kernel,
        grid=(B,),
        in_specs=[pl.BlockSpec((1, L, LANES), lambda b: (b, 0, 0))],
        out_specs=pl.BlockSpec((1, L, width), lambda b: (b, 0, 0)),
        out_shape=jax.ShapeDtypeStruct((B, L, width), BF16),
        compiler_params=pltpu.CompilerParams(
            dimension_semantics=("arbitrary",), vmem_limit_bytes=VMEM_LIMIT),
        name="fox_ext",
    )(cum_col)


def _fox_attn_kernel(q_ref, k_ref, x_ref, vt_ref, g_ref, c_ref, o_ref):
    i = pl.program_id(2)
    n_p = q_ref.shape[2] // LANES
    pairs = range(n_p)
    ps = [slice(p * LANES, (p + 1) * LANES) for p in pairs]
    lane = lax.broadcasted_iota(jnp.int32, (1, LANES), 1)
    is_a = lane < HEAD
    zq = jnp.zeros((), BF16)

    def ext_row(cref, h):
        hi, mid, lo = (t.astype(F32) for t in _split3(cref))
        row = jnp.where((lane >= 3 * h) & (lane < 3 * h + 3), 1.0, 0.0)
        row = row + jnp.where(lane == 6, hi, 0.0) + jnp.where(lane == 7, mid, 0.0) \
            + jnp.where(lane == 8, lo, 0.0)
        return jnp.broadcast_to(row, (BLK, LANES)).astype(BF16)

    qx = []
    for p in pairs:
        q2 = q_ref[0, :, ps[p]]
        cref = c_ref[0, p, i][:, 0:1] * LOG2E
        qx.append(jnp.concatenate(
            [jnp.concatenate([jnp.where(is_a, q2, zq), ext_row(cref[0:1], 0)], axis=1),
             jnp.concatenate([jnp.where(is_a, zq, q2), ext_row(cref[1:2], 1)], axis=1)], axis=0))

    key_i = lax.broadcasted_iota(jnp.int32, (BLK, 2 * BLK), 0)
    qry_i = lax.broadcasted_iota(jnp.int32, (BLK, 2 * BLK), 1) % BLK
    keep = key_i <= qry_i

    def step(j, carry, nb=1):
        m, l, acc = carry
        off = pl.multiple_of(j * BLK, BLK)
        unmasked = jnp.logical_or(keep, j + (nb - 1) != i)
        m, l, acc = list(m), list(l), list(acc)
        n_sub = ATTN_SUBGROUP if nb > 1 else n_p
        for g0 in range(0, n_p, n_sub):
            sub = range(g0, min(g0 + n_sub, n_p))
            st = {}
            for p in sub:
                kx = jnp.concatenate([k_ref[0, pl.ds(off, nb * BLK), ps[p]],
                                      x_ref[0, pl.ds(off, nb * BLK), ps[p]]], axis=1)
                s = _dot_nt(kx, qx[p])
                last = jnp.where(unmasked, s[(nb - 1) * BLK:], NEG)
                st[p] = last if nb == 1 else jnp.concatenate([s[:(nb - 1) * BLK], last], axis=0)
            for p in sub:
                mn = jnp.maximum(m[p], jnp.max(st[p], axis=0, keepdims=True))
                alpha = jnp.exp2(m[p] - mn)
                pr = jnp.exp2(st[p] - mn)
                l[p] = l[p] * alpha + jnp.sum(pr, axis=0, keepdims=True)
                vt = jnp.concatenate([vt_ref[0, j + t, ps[p], :] for t in range(nb)], axis=1)
                acc[p] = acc[p] * alpha + _dot(vt, pr.astype(BF16))
                m[p] = mn
        return tuple(m), tuple(l), tuple(acc)

    n_grp = (i + 1) // KV_GROUP
    carry = (tuple(jnp.full((1, 2 * BLK), NEG, F32) for _ in pairs),
             tuple(jnp.zeros((1, 2 * BLK), F32) for _ in pairs),
             tuple(jnp.zeros((LANES, 2 * BLK), F32) for _ in pairs))
    carry = lax.fori_loop(0, n_grp, lambda jj, c: step(jj * KV_GROUP, c, KV_GROUP), carry)
    _, l, acc = lax.fori_loop(n_grp * KV_GROUP, i + 1, step, carry)
    for p in pairs:
        o_t = acc[p] / l[p]
        o = jnp.concatenate([o_t[:HEAD, :BLK], o_t[HEAD:, BLK:]], axis=0).T
        g = g_ref[0, :, ps[p]].astype(F32)
        o_ref[0, :, ps[p]] = (o * (g * _sigmoid(g))).astype(BF16)


def _fox_attn(p3, kext, v_t, cum5):
    B, L, _ = p3.shape
    n_pair = (p3.shape[2] // 4) // LANES
    nq = L // BLK
    n_p = ATTN_PAIRS
    width = n_p * LANES
    n_grp = n_pair // n_p
    return pl.pallas_call(
        _fox_attn_kernel,
        grid=(B, n_grp, nq),
        in_specs=[
            pl.BlockSpec((1, BLK, width), lambda b, p, i: (b, i, p)),
            pl.BlockSpec((1, L, width), lambda b, p, i: (b, 0, n_grp + p)),
            pl.BlockSpec((1, L, width), lambda b, p, i: (b, 0, p)),
            pl.BlockSpec((1, nq, width, BLK), lambda b, p, i: (b, 0, p, 0)),
            pl.BlockSpec((1, BLK, width), lambda b, p, i: (b, i, 3 * n_grp + p)),
            pl.BlockSpec((1, n_p, nq, 2, BLK), lambda b, p, i: (b, p, 0, 0, 0)),
        ],
        out_specs=pl.BlockSpec((1, BLK, width), lambda b, p, i: (b, i, p)),
        out_shape=jax.ShapeDtypeStruct((B, L, n_pair * LANES), BF16),
        compiler_params=pltpu.CompilerParams(
            dimension_semantics=("arbitrary", "arbitrary", "arbitrary"), vmem_limit_bytes=VMEM_LIMIT),
        name="fox_attn",
    )(p3, p3, kext, v_t, p3, cum5)


def _head_sums(x):
    row = lax.broadcasted_iota(jnp.int32, (LANES, LANES), 0)
    col = lax.broadcasted_iota(jnp.int32, (LANES, LANES), 1)
    ones = ((row // HEAD) == (col // HEAD)).astype(BF16)
    parts = []
    for g in range(x.shape[1] // LANES):
        parts.append(_dot(x[:, g * LANES:(g + 1) * LANES].astype(BF16), ones))
    return jnp.concatenate(parts, axis=1)


def _wkv_kernel(r_ref, k_ref, v_ref, g_ref, rh_ref, kh_ref, vh_ref, gh_ref, t_ref, th_ref,
                mu_ref, mut_ref, w0_ref, wup_ref, a0_ref, aup_ref, kk_ref, ka_ref, rk_ref,
                lnw_ref, lnb_ref, o_ref, st_ref):
    c = pl.program_id(1)

    @pl.when(c == 0)
    def _():
        st_ref[...] = jnp.zeros_like(st_ref)

    n_b = r_ref.shape[0]
    D = r_ref.shape[2]
    n_pair = D // LANES
    not_first = c != 0
    row = lax.broadcasted_iota(jnp.int32, (CHUNK, CHUNK), 0)
    col = lax.broadcasted_iota(jnp.int32, (CHUNK, CHUNK), 1)
    sub_diag = (col + 1 == row).astype(BF16)
    tril = (col <= row).astype(BF16)
    top0 = (lax.broadcasted_iota(jnp.int32, (8, 1), 0) == 0) & not_first
    trow0 = lax.broadcasted_iota(jnp.int32, (CHUNK, 1), 0) == 0

    def lerp(cur, sh, prev, mu):
        sh = jnp.concatenate([sh[0:8] + jnp.where(top0, prev, 0.0), sh[8:]], axis=0)
        return cur + (sh - cur) * mu

    def shifted(cur, prev, mu):
        return lerp(cur.astype(F32), _dot(sub_diag, cur), prev.astype(F32), mu)

    def prep(b):
        r = shifted(r_ref[b], rh_ref[b, 7:8, :], mu_ref[:, 0:D])
        k = shifted(k_ref[b], kh_ref[b, 7:8, :], mu_ref[:, D:2 * D])
        v = shifted(v_ref[b], vh_ref[b, 7:8, :], mu_ref[:, 2 * D:3 * D])
        g = shifted(g_ref[b], gh_ref[b, 7:8, :], mu_ref[:, 3 * D:4 * D])
        tcur = t_ref[b]
        tl = lerp(tcur, jnp.where(trow0, 0.0, pltpu.roll(tcur, 1, axis=0)), th_ref[b, 7:8, :],
                  mut_ref[...])
        w_log = w0_ref[...] + _dot(jnp.tanh(tl[:, :HEAD]).astype(BF16), wup_ref[...])
        lw = (-0.5 * DECAY_SCALE * LOG2E) * jnp.tanh(0.5 * w_log) - 0.5 * DECAY_SCALE * LOG2E
        a = _sigmoid(a0_ref[...] + _dot(tl[:, HEAD:].astype(BF16), aup_ref[...]))
        kk = k * kk_ref[...]
        kk = kk * lax.rsqrt(jnp.maximum(_head_sums(kk * kk), 1e-24))
        k = k * (1.0 + (a - 1.0) * ka_ref[...])
        hi, mid, lo = _split3(lw)
        cl = _dot(tril, hi) + _dot(tril, mid) + _dot(tril, lo)
        e_neg = jnp.exp2(-cl)
        return dict(
            r=r, k=k, v=v, g=g,
            r_t=(r * jnp.exp2(cl)).astype(BF16),
            a_t=(-kk * jnp.exp2(cl - lw)).astype(BF16),
            b_t=(kk * a * e_neg).astype(BF16),
            k_t=(k * e_neg).astype(BF16),
            w_end=jnp.exp2(cl[CHUNK - 1:CHUNK, :]),
            v_b=v.astype(BF16))

    pre = [prep(b) for b in range(n_b)]

    lane = lax.broadcasted_iota(jnp.int32, (1, LANES), 1)
    is_a = lane < HEAD
    trw = lax.broadcasted_iota(jnp.int32, (CHUNK, LANES), 0)
    jw = lax.broadcasted_iota(jnp.int32, (CHUNK, LANES), 1) % HEAD
    strict = jw < trw
    incl = jw <= trw
    r2 = lax.broadcasted_iota(jnp.int32, (LANES, LANES), 0) // HEAD
    c2 = lax.broadcasted_iota(jnp.int32, (LANES, LANES), 1) // HEAD
    same_head = r2 == c2
    zb16 = jnp.zeros((), BF16)

    def bdiag(xw, swap=False):
        za = jnp.where(is_a, xw, jnp.zeros_like(xw))
        zb = jnp.where(is_a, jnp.zeros_like(xw), xw)
        return jnp.concatenate([zb, za] if swap else [za, zb], axis=0)

    streams = [(b, p) for b in range(n_b) for p in range(n_pair)]
    ns = range(len(streams))

    def part(name, n):
        b, p = streams[n]
        return pre[b][name][:, p * LANES:(p + 1) * LANES]

    xq = [jnp.concatenate([part("a_t", n), part("r_t", n)], axis=0) for n in ns]
    ycat = [jnp.concatenate(
        [jnp.where(is_a, jnp.concatenate([part("k_t", n), part("b_t", n)], axis=0), zb16),
         jnp.where(is_a, zb16, jnp.concatenate([part("b_t", n), part("k_t", n)], axis=0))],
        axis=0) for n in ns]
    sc = [_dot_nt(xq[n], ycat[n]) for n in ns]
    s0 = [st_ref[b, p] for b, p in streams]
    xs = [_dot_nt(xq[n], s0[n].astype(BF16)) for n in ns]
    top_lo = [sc[n][:CHUNK, :LANES] for n in ns]
    top_hi = [sc[n][:CHUNK, LANES:] for n in ns]
    bot_lo = [sc[n][CHUNK:, :LANES] for n in ns]
    bot_hi = [sc[n][CHUNK:, LANES:] for n in ns]
    a_ak = [jnp.where(strict, jnp.where(is_a, top_lo[n], top_hi[n]), 0.0).astype(BF16) for n in ns]
    a_ab = [jnp.where(strict, jnp.where(is_a, top_hi[n], top_lo[n]), 0.0) for n in ns]
    m_rk = [jnp.where(incl, jnp.where(is_a, bot_lo[n], bot_hi[n]), 0.0).astype(BF16) for n in ns]
    m_rb = [jnp.where(incl, jnp.where(is_a, bot_hi[n], bot_lo[n]), 0.0).astype(BF16) for n in ns]

    eye = (jw == trw).astype(F32)
    lvl1 = (trw // 2 == jw // 2) & strict
    tw = [eye + jnp.where(lvl1, a_ab[n], 0.0) for n in ns]
    s = 2
    while s < CHUNK:
        off = (trw // (2 * s) == jw // (2 * s)) & ((trw // s) % 2 == 1) & ((jw // s) % 2 == 0)
        tb = [t.astype(BF16) for t in tw]
        pm = [_dot(tb[n], bdiag(jnp.where(off, a_ab[n], 0.0).astype(BF16))) for n in ns]
        tw = [tw[n] + _dot(pm[n].astype(BF16), bdiag(tb[n])) for n in ns]
        s *= 2

    vbd = [bdiag(part("v_b", n)) for n in ns]
    gm = [xs[n][:CHUNK] + _dot(a_ak[n], vbd[n]) for n in ns]
    u = [_dot(tw[n].astype(BF16), bdiag(gm[n].astype(BF16), swap=True)).astype(BF16) for n in ns]
    ys = [xs[n][CHUNK:] + _dot(jnp.concatenate([m_rb[n], m_rk[n]], axis=1),
                               jnp.concatenate([bdiag(u[n], swap=True), vbd[n]], axis=0))
          for n in ns]
    for n in ns:
        b, p = streams[n]
        uv = jnp.concatenate([u[n], part("v_b", n)], axis=0)
        bk = jnp.concatenate([part("b_t", n), part("k_t", n)], axis=0)
        st_ref[b, p] = (s0[n] + jnp.where(same_head, _dot_tn(uv, bk), 0.0)) * part("w_end", n)

    inv_n = 1.0 / HEAD
    for b in range(n_b):
        y = jnp.concatenate(ys[b * n_pair:(b + 1) * n_pair], axis=1)
        q = pre[b]
        mean = _head_sums(y) * inv_n
        d = y - mean
        var = _head_sums(d * d) * inv_n
        yn = d * lax.rsqrt(var + GN_EPS) * lnw_ref[...] + lnb_ref[...]
        bonus = _head_sums(q["r"] * q["k"] * rk_ref[...]) * q["v"]
        o_ref[b] = ((yn + bonus) * (q["g"] * _sigmoid(q["g"]))).astype(BF16)


def _wkv(p3, tail3, mu, mu_t, w0, w_up, a0, a_up, k_k, k_a, r_k, ln_w, ln_b):
    B, L, n_main = p3.shape
    D = n_main // 4
    nc = L // CHUNK
    sub = CHUNK // 8
    n_b = WKV_BATCH if B % WKV_BATCH == 0 else 1

    def cur(s):
        return pl.BlockSpec((n_b, CHUNK, D), lambda b, c: (b, c, s))

    def halo(s):
        return pl.BlockSpec((n_b, 8, D), lambda b, c: (b, jnp.maximum(c * sub - 1, 0), s))

    def vec(n):
        return pl.BlockSpec((1, n), lambda b, c: (0, 0))

    lora = pl.BlockSpec((HEAD, D), lambda b, c: (0, 0))
    return pl.pallas_call(
        _wkv_kernel,
        grid=(B // n_b, nc),
        in_specs=[cur(0), cur(1), cur(2), cur(3), halo(0), halo(1), halo(2), halo(3),
                  pl.BlockSpec((n_b, CHUNK, LANES), lambda b, c: (b, c, 0)),
                  pl.BlockSpec((n_b, 8, LANES), lambda b, c: (b, jnp.maximum(c * sub - 1, 0), 0)),
                  vec(n_main), vec(LANES), vec(D), lora, vec(D), lora,
                  vec(D), vec(D), vec(D), vec(D), vec(D)],
        out_specs=pl.BlockSpec((n_b, CHUNK, D), lambda b, c: (b, c, 0)),
        out_shape=jax.ShapeDtypeStruct((B, L, D), BF16),
        scratch_shapes=[pltpu.VMEM((n_b, D // LANES, LANES, LANES), F32)],
        compiler_params=pltpu.CompilerParams(
            dimension_semantics=("arbitrary", "arbitrary"), vmem_limit_bytes=VMEM_LIMIT),
        name="wkv7",
    )(p3, p3, p3, p3, p3, p3, p3, p3, tail3, tail3,
      mu, mu_t, w0, w_up, a0, a_up, k_k, k_a, r_k, ln_w, ln_b)


def _fox_mixer(p, tail, B, L, b_f):
    D = p.shape[1] // 4
    n_heads = b_f.shape[0]
    f_t = tail[:, :n_heads].reshape(B, L, n_heads).transpose(0, 2, 1).reshape(B * n_heads, L)
    b_col = jnp.tile(b_f, B).reshape(B * n_heads, 1)
    cum = _fox_cum(f_t, b_col)
    cum3 = cum.reshape(B, n_heads, L)
    cum5 = cum3.reshape(B, n_heads // 2, 2, L // BLK, BLK).transpose(0, 1, 3, 2, 4)
    cum_col = jnp.pad(cum3.transpose(0, 2, 1), ((0, 0), (0, 0), (0, LANES - n_heads)))
    p3 = p.reshape(B, L, 4 * D)
    v_t = p3[:, :, 2 * D:3 * D].reshape(B, L // BLK, BLK, D).transpose(0, 1, 3, 2)
    return _fox_attn(p3, _fox_ext(cum_col, D), v_t, cum5).reshape(B * L, D)


def _rwkv_mixer(p, tail, B, L, mu, w0, w_up, a0, a_up, k_k, k_a, r_k, ln_w, ln_b):
    n_main = p.shape[1]
    row = lambda t: t.reshape(1, -1)
    z = _wkv(p.reshape(B, L, n_main), tail.reshape(B, L, LANES),
             row(mu[:n_main]), row(mu[n_main:]), row(w0), w_up.astype(BF16), row(a0),
             a_up.astype(BF16), row(k_k), row(k_a), row(r_k), row(ln_w), row(ln_b))
    return z.reshape(B * L, n_main // 4)


def kernel(x, meta_tokens, norm_pre, norm_post, fox_w_in, fox_b_f, fox_w_out, rwkv_w_in, rwkv_mu, rwkv_w0, rwkv_w_up, rwkv_a0, rwkv_a_up, rwkv_k_k, rwkv_k_a, rwkv_r_k, rwkv_ln_w, rwkv_ln_b, rwkv_w_out):
    B, S, D = x.shape
    depth = norm_pre.shape[0]
    n_heads = fox_b_f.shape[1]
    L0 = N_META + S
    L = -(-L0 // BLK) * BLK
    assert (B * L) % ROW_TILE == 0 and D % LANES == 0
    meta = jnp.broadcast_to(meta_tokens[None].astype(x.dtype), (B, N_META, D))
    h = jnp.concatenate([meta, x, jnp.zeros((B, L - L0, D), x.dtype)], axis=1).reshape(B * L, D)

    def in_proj_args(i):
        g = norm_pre[i].reshape(1, D)
        if i % 2 == 0:
            w = jnp.concatenate([fox_w_in[i // 2], jnp.zeros((D, LANES - n_heads), F32)], axis=1)
            return w.astype(BF16), g, dict(q_cols=D, q_scale=LOG2E * HEAD ** -0.5)
        return rwkv_w_in[i // 2].astype(BF16), g, {}

    w, g, qkw = in_proj_args(0)
    p, tail = _norm_proj(h, g, w, **qkw)
    for i in range(depth):
        j = i // 2
        if i % 2 == 0:
            z = _fox_mixer(p, tail, B, L, fox_b_f[j])
            w_out = fox_w_out[j]
        else:
            z = _rwkv_mixer(p, tail, B, L, rwkv_mu[j], rwkv_w0[j], rwkv_w_up[j], rwkv_a0[j],
                            rwkv_a_up[j], rwkv_k_k[j], rwkv_k_a[j], rwkv_r_k[j], rwkv_ln_w[j],
                            rwkv_ln_b[j])
            w_out = rwkv_w_out[j]
        g_post = norm_post[i].reshape(1, D)
        if i + 1 < depth:
            w, g, qkw = in_proj_args(i + 1)
            h, p, tail = _out_norm_proj(z, w_out.astype(BF16), h, g_post, g, w, **qkw)
        else:
            h = _out_proj(z, w_out.astype(BF16), h, g_post)
    return h.reshape(B, L, D)[:, N_META:L0]
```

```python
import functools
import math

import jax
import jax.numpy as jnp
from jax import lax
from jax.experimental import pallas as pl
from jax.experimental.pallas import tpu as pltpu

F32 = jnp.float32
BF16 = jnp.bfloat16

N_META = 16
HEAD = 64
LANES = 128
BLK = 128
ATTN_PAIRS = 8
ATTN_SUBGROUP = 8
KV_GROUP = 4
SUM_ROWS = 16
CHUNK = 64
WKV_BATCH = 4
ROW_TILE = 512
COL_TILE = 512
NORM_EPS = 1e-6
GN_EPS = 64e-5
DECAY_SCALE = math.exp(-0.5)
LOG2E = math.log2(math.e)
NEG = -1e30
VMEM_LIMIT = 48 * 1024 * 1024
VMEM_LIMIT_FUSED = 56 * 1024 * 1024


def _dot(a, b):
    return jnp.dot(a, b, preferred_element_type=F32)


def _dot_nt(a, b):
    return lax.dot_general(a, b, (((1,), (1,)), ((), ())), preferred_element_type=F32)


def _dot_tn(a, b):
    return lax.dot_general(a, b, (((0,), (0,)), ((), ())), preferred_element_type=F32)


def _split3(x):
    hi = x.astype(BF16)
    r1 = x - hi.astype(F32)
    mid = r1.astype(BF16)
    lo = (r1 - mid.astype(F32)).astype(BF16)
    return hi, mid, lo


def _sigmoid(x):
    return 0.5 * jnp.tanh(0.5 * x) + 0.5


def _project(x, g_ref, w_ref, o_ref, t_ref, n_main, q_cols, q_scale):
    var = jnp.mean(x * x, axis=-1, keepdims=True)
    u = (x * lax.rsqrt(var + NORM_EPS) * g_ref[...]).astype(BF16)
    for j in range(0, n_main, COL_TILE):
        acc = _dot(u, w_ref[:, j:j + COL_TILE])
        if j < q_cols:
            acc = acc * q_scale
        o_ref[:, j:j + COL_TILE] = acc.astype(BF16)
    t_ref[...] = _dot(u, w_ref[:, n_main:])


def _norm_proj_kernel(h_ref, g_ref, w_ref, o_ref, t_ref, *, n_main, q_cols, q_scale):
    _project(h_ref[...], g_ref, w_ref, o_ref, t_ref, n_main, q_cols, q_scale)


def _norm_proj(h, g, w, *, q_cols=0, q_scale=1.0):
    T, D = h.shape
    n_all = w.shape[1]
    n_main = n_all - LANES
    kern = functools.partial(_norm_proj_kernel, n_main=n_main, q_cols=q_cols, q_scale=q_scale)
    return pl.pallas_call(
        kern,
        grid=(T // ROW_TILE,),
        in_specs=[
            pl.BlockSpec((ROW_TILE, D), lambda i: (i, 0)),
            pl.BlockSpec((1, D), lambda i: (0, 0)),
            pl.BlockSpec((D, n_all), lambda i: (0, 0)),
        ],
        out_specs=[
            pl.BlockSpec((ROW_TILE, n_main), lambda i: (i, 0)),
            pl.BlockSpec((ROW_TILE, LANES), lambda i: (i, 0)),
        ],
        out_shape=[
            jax.ShapeDtypeStruct((T, n_main), BF16),
            jax.ShapeDtypeStruct((T, LANES), F32),
        ],
        compiler_params=pltpu.CompilerParams(
            dimension_semantics=("arbitrary",), vmem_limit_bytes=VMEM_LIMIT),
        name="norm_proj",
    )(h, g, w)


def _residual(z_ref, w_ref, h_ref, g_ref):
    m = _dot(z_ref[...], w_ref[...])
    var = jnp.mean(m * m, axis=-1, keepdims=True)
    return h_ref[...] + m * lax.rsqrt(var + NORM_EPS) * g_ref[...]


def _out_proj_kernel(z_ref, w_ref, h_ref, g_ref, o_ref):
    o_ref[...] = _residual(z_ref, w_ref, h_ref, g_ref)


def _out_norm_proj_kernel(z_ref, wo_ref, h_ref, gpost_ref, gpre_ref, wi_ref, ho_ref, o_ref, t_ref,
                          *, n_main, q_cols, q_scale):
    hn = _residual(z_ref, wo_ref, h_ref, gpost_ref)
    ho_ref[...] = hn
    _project(hn, gpre_ref, wi_ref, o_ref, t_ref, n_main, q_cols, q_scale)


def _out_norm_proj(z, w_out, h, g_post, g_pre, w_in, *, q_cols=0, q_scale=1.0):
    T, D = h.shape
    n_all = w_in.shape[1]
    n_main = n_all - LANES
    kern = functools.partial(_out_norm_proj_kernel, n_main=n_main, q_cols=q_cols, q_scale=q_scale)
    rows = lambda n: pl.BlockSpec((ROW_TILE, n), lambda i: (i, 0))
    whole = lambda a: pl.BlockSpec(a.shape, lambda i: (0, 0))
    return pl.pallas_call(
        kern,
        grid=(T // ROW_TILE,),
        in_specs=[rows(D), whole(w_out), rows(D), whole(g_post), whole(g_pre), whole(w_in)],
        out_specs=[rows(D), rows(n_main), rows(LANES)],
        out_shape=[
            jax.ShapeDtypeStruct((T, D), F32),
            jax.ShapeDtypeStruct((T, n_main), BF16),
            jax.ShapeDtypeStruct((T, LANES), F32),
        ],
        compiler_params=pltpu.CompilerParams(
            dimension_semantics=("arbitrary",), vmem_limit_bytes=VMEM_LIMIT_FUSED),
        name="out_norm_proj",
    )(z, w_out, h, g_post, g_pre, w_in)


def _out_proj(z, w, h, g):
    T, D = h.shape
    return pl.pallas_call(
        _out_proj_kernel,
        grid=(T // ROW_TILE,),
        in_specs=[
            pl.BlockSpec((ROW_TILE, D), lambda i: (i, 0)),
            pl.BlockSpec((D, D), lambda i: (0, 0)),
            pl.BlockSpec((ROW_TILE, D), lambda i: (i, 0)),
            pl.BlockSpec((1, D), lambda i: (0, 0)),
        ],
        out_specs=pl.BlockSpec((ROW_TILE, D), lambda i: (i, 0)),
        out_shape=jax.ShapeDtypeStruct((T, D), F32),
        compiler_params=pltpu.CompilerParams(
            dimension_semantics=("arbitrary",), vmem_limit_bytes=VMEM_LIMIT),
        name="out_proj",
    )(z, w, h, g)


def _fox_bias_kernel(t_ref, b_ref, x_ref, q_ref):
    L = t_ref.shape[1]
    width = x_ref.shape[2]
    n_heads = q_ref.shape[2]
    x = t_ref[0] + b_ref[...]
    lf = jnp.minimum(x, 0.0) - jnp.log(1.0 + jnp.exp(-jnp.abs(x)))
    row = lax.broadcasted_iota(jnp.int32, (BLK, BLK), 0)
    col = lax.broadcasted_iota(jnp.int32, (BLK, BLK), 1)
    tril = (col <= row).astype(BF16)
    carry = jnp.zeros((1, LANES), F32)
    blocks = []
    for j in range(L // BLK):
        hi, mid, lo = _split3(lf[j * BLK:(j + 1) * BLK])
        blk = _dot(tril, hi) + _dot(tril, mid) + _dot(tril, lo) + carry
        carry = blk[BLK - 1:BLK]
        blocks.append(blk)
    cum2 = jnp.concatenate(blocks, axis=0) * LOG2E

    head = lax.broadcasted_iota(jnp.int32, (LANES, width), 0)
    colw = lax.broadcasted_iota(jnp.int32, (LANES, width), 1)
    base = (head // 2) * LANES + (head % 2) * 3
    ext = jnp.zeros((L, width), F32)
    for o, term in enumerate(_split3(-cum2)):
        ext = ext + _dot(term, ((head < n_heads) & (colw == base + o)).astype(BF16))
    lane_w = lax.broadcasted_iota(jnp.int32, (1, width), 1) % LANES
    x_ref[0] = jnp.where((lane_w >= 6) & (lane_w < 9), 1.0, ext).astype(BF16)

    hrow = lax.broadcasted_iota(jnp.int32, (n_heads, LANES), 0)
    lane = lax.broadcasted_iota(jnp.int32, (n_heads, LANES), 1)
    slot = (hrow % 2) * 3
    own = ((lane >= slot) & (lane < slot + 3)).astype(F32)
    for j in range(L // BLK):
        diag = jnp.where(hrow == lane, blocks[j][0:1] * LOG2E, 0.0)
        rows = own
        for o, term in enumerate(_split3(diag)):
            rows = rows + _dot(term, (col == 6 + o).astype(BF16))
        q_ref[0, j] = rows


def _fox_bias(tail3, b_row, n_heads, width):
    B, L, _ = tail3.shape
    nq = L // BLK
    return pl.pallas_call(
        _fox_bias_kernel,
        grid=(B,),
        in_specs=[pl.BlockSpec((1, L, LANES), lambda b: (b, 0, 0)),
                  pl.BlockSpec((1, LANES), lambda b: (0, 0))],
        out_specs=[pl.BlockSpec((1, L, width), lambda b: (b, 0, 0)),
                   pl.BlockSpec((1, nq, n_heads, LANES), lambda b: (b, 0, 0, 0))],
        out_shape=[jax.ShapeDtypeStruct((B, L, width), BF16),
                   jax.ShapeDtypeStruct((B, nq, n_heads, LANES), F32)],
        compiler_params=pltpu.CompilerParams(
            dimension_semantics=("arbitrary",), vmem_limit_bytes=VMEM_LIMIT),
        name="fox_bias",
    )(tail3, b_row)


def _fox_attn_kernel(q_ref, k_ref, x_ref, vt_ref, g_ref, qx_ref, o_ref):
    i = pl.program_id(2)
    n_p = q_ref.shape[2] // LANES
    pairs = range(n_p)
    ps = [slice(p * LANES, (p + 1) * LANES) for p in pairs]
    lane = lax.broadcasted_iota(jnp.int32, (1, LANES), 1)
    is_a = lane < HEAD
    zq = jnp.zeros((), BF16)
    ones_rows = jnp.ones((SUM_ROWS, BLK), BF16)

    def ext_rows(h):
        return jnp.broadcast_to(qx_ref[0, 0, h:h + 1, :], (BLK, LANES)).astype(BF16)

    qx = []
    for p in pairs:
        q2 = q_ref[0, :, ps[p]]
        qx.append(jnp.concatenate(
            [jnp.concatenate([jnp.where(is_a, q2, zq), ext_rows(2 * p)], axis=1),
             jnp.concatenate([jnp.where(is_a, zq, q2), ext_rows(2 * p + 1)], axis=1)], axis=0))

    key_i = lax.broadcasted_iota(jnp.int32, (BLK, 2 * BLK), 0)
    qry_i = lax.broadcasted_iota(jnp.int32, (BLK, 2 * BLK), 1) % BLK
    keep = key_i <= qry_i

    def step(j, carry, nb=1):
        m, acc = carry
        off = pl.multiple_of(j * BLK, BLK)
        unmasked = jnp.logical_or(keep, j + (nb - 1) != i)
        m, acc = list(m), list(acc)
        n_sub = ATTN_SUBGROUP if nb > 1 else n_p
        for g0 in range(0, n_p, n_sub):
            sub = range(g0, min(g0 + n_sub, n_p))
            st = {}
            for p in sub:
                kx = jnp.concatenate([k_ref[0, pl.ds(off, nb * BLK), ps[p]],
                                      x_ref[0, pl.ds(off, nb * BLK), ps[p]]], axis=1)
                s = _dot_nt(kx, qx[p])
                last = jnp.where(unmasked, s[(nb - 1) * BLK:], NEG)
                st[p] = last if nb == 1 else jnp.concatenate([s[:(nb - 1) * BLK], last], axis=0)
            for p in sub:
                mn = jnp.maximum(m[p], jnp.max(st[p], axis=0, keepdims=True))
                alpha = jnp.exp2(m[p] - mn)
                pr = jnp.exp2(st[p] - mn).astype(BF16)
                vt = jnp.concatenate(
                    [jnp.concatenate([vt_ref[0, j + t, ps[p], :], ones_rows], axis=0)
                     for t in range(nb)], axis=1)
                acc[p] = acc[p] * alpha + _dot(vt, pr)
                m[p] = mn
        return tuple(m), tuple(acc)

    n_grp = (i + 1) // KV_GROUP
    carry = (tuple(jnp.full((1, 2 * BLK), NEG, F32) for _ in pairs),
             tuple(jnp.zeros((LANES + SUM_ROWS, 2 * BLK), F32) for _ in pairs))
    carry = lax.fori_loop(0, n_grp, lambda jj, c: step(jj * KV_GROUP, c, KV_GROUP), carry)
    _, acc = lax.fori_loop(n_grp * KV_GROUP, i + 1, step, carry)
    for p in pairs:
        o_t = acc[p][:LANES] / acc[p][LANES:LANES + 1]
        o = jnp.concatenate([o_t[:HEAD, :BLK], o_t[HEAD:LANES, BLK:]], axis=0).T
        g = g_ref[0, :, ps[p]].astype(F32)
        o_ref[0, :, ps[p]] = (o * (g * _sigmoid(g))).astype(BF16)


def _fox_attn(p3, kext, v_t, qrows):
    B, L, _ = p3.shape
    n_pair = (p3.shape[2] // 4) // LANES
    nq = L // BLK
    n_p = ATTN_PAIRS
    width = n_p * LANES
    n_grp = n_pair // n_p
    assert n_grp == 1 and qrows.shape[2] == 2 * n_pair
    return pl.pallas_call(
        _fox_attn_kernel,
        grid=(B, n_grp, nq),
        in_specs=[
            pl.BlockSpec((1, BLK, width), lambda b, p, i: (b, i, p)),
            pl.BlockSpec((1, L, width), lambda b, p, i: (b, 0, n_grp + p)),
            pl.BlockSpec((1, L, width), lambda b, p, i: (b, 0, p)),
            pl.BlockSpec((1, nq, width, BLK), lambda b, p, i: (b, 0, p, 0)),
            pl.BlockSpec((1, BLK, width), lambda b, p, i: (b, i, 3 * n_grp + p)),
            pl.BlockSpec((1, 1, 2 * n_pair, LANES), lambda b, p, i: (b, i, 0, 0)),
        ],
        out_specs=pl.BlockSpec((1, BLK, width), lambda b, p, i: (b, i, p)),
        out_shape=jax.ShapeDtypeStruct((B, L, n_pair * LANES), BF16),
        compiler_params=pltpu.CompilerParams(
            dimension_semantics=("arbitrary", "arbitrary", "arbitrary"), vmem_limit_bytes=VMEM_LIMIT),
        name="fox_attn",
    )(p3, p3, kext, v_t, p3, qrows)


def _head_sums(x):
    row = lax.broadcasted_iota(jnp.int32, (LANES, LANES), 0)
    col = lax.broadcasted_iota(jnp.int32, (LANES, LANES), 1)
    ones = ((row // HEAD) == (col // HEAD)).astype(BF16)
    parts = []
    for g in range(x.shape[1] // LANES):
        parts.append(_dot(x[:, g * LANES:(g + 1) * LANES].astype(BF16), ones))
    return jnp.concatenate(parts, axis=1)


def _wkv_kernel(r_ref, k_ref, v_ref, g_ref, rh_ref, kh_ref, vh_ref, gh_ref, t_ref, th_ref,
                mu_ref, mut_ref, w0_ref, wup_ref, a0_ref, aup_ref, kk_ref, ka_ref, rk_ref,
                lnw_ref, lnb_ref, o_ref, st_ref):
    c = pl.program_id(1)

    @pl.when(c == 0)
    def _():
        st_ref[...] = jnp.zeros_like(st_ref)

    n_b = r_ref.shape[0]
    D = r_ref.shape[2]
    n_pair = D // LANES
    not_first = c != 0
    row = lax.broadcasted_iota(jnp.int32, (CHUNK, CHUNK), 0)
    col = lax.broadcasted_iota(jnp.int32, (CHUNK, CHUNK), 1)
    sub_diag = (col + 1 == row).astype(BF16)
    tril = (col <= row).astype(BF16)
    top0 = (lax.broadcasted_iota(jnp.int32, (8, 1), 0) == 0) & not_first
    trow0 = lax.broadcasted_iota(jnp.int32, (CHUNK, 1), 0) == 0

    def lerp(cur, sh, prev, mu):
        sh = jnp.concatenate([sh[0:8] + jnp.where(top0, prev, 0.0), sh[8:]], axis=0)
        return cur + (sh - cur) * mu

    def shifted(cur, prev, mu):
        return lerp(cur.astype(F32), _dot(sub_diag, cur), prev.astype(F32), mu)

    def prep(b):
        r = shifted(r_ref[b], rh_ref[b, 7:8, :], mu_ref[:, 0:D])
        k = shifted(k_ref[b], kh_ref[b, 7:8, :], mu_ref[:, D:2 * D])
        v = shifted(v_ref[b], vh_ref[b, 7:8, :], mu_ref[:, 2 * D:3 * D])
        g = shifted(g_ref[b], gh_ref[b, 7:8, :], mu_ref[:, 3 * D:4 * D])
        tcur = t_ref[b]
        tl = lerp(tcur, jnp.where(trow0, 0.0, pltpu.roll(tcur, 1, axis=0)), th_ref[b, 7:8, :],
                  mut_ref[...])
        w_log = w0_ref[...] + _dot(jnp.tanh(tl[:, :HEAD]).astype(BF16), wup_ref[...])
        lw = (-0.5 * DECAY_SCALE * LOG2E) * jnp.tanh(0.5 * w_log) - 0.5 * DECAY_SCALE * LOG2E
        a = _sigmoid(a0_ref[...] + _dot(tl[:, HEAD:].astype(BF16), aup_ref[...]))
        kk = k * kk_ref[...]
        kk = kk * lax.rsqrt(jnp.maximum(_head_sums(kk * kk), 1e-24))
        k = k * (1.0 + (a - 1.0) * ka_ref[...])
        hi, mid, lo = _split3(lw)
        cl = _dot(tril, hi) + _dot(tril, mid) + _dot(tril, lo)
        e_neg = jnp.exp2(-cl)
        return dict(
            r=r, k=k, v=v, g=g,
            r_t=(r * jnp.exp2(cl)).astype(BF16),
            a_t=(-kk * jnp.exp2(cl - lw)).astype(BF16),
            b_t=(kk * a * e_neg).astype(BF16),
            k_t=(k * e_neg).astype(BF16),
            w_end=jnp.exp2(cl[CHUNK - 1:CHUNK, :]),
            v_b=v.astype(BF16))

    pre = [prep(b) for b in range(n_b)]

    lane = lax.broadcasted_iota(jnp.int32, (1, LANES), 1)
    is_a = lane < HEAD
    trw = lax.broadcasted_iota(jnp.int32, (CHUNK, LANES), 0)
    jw = lax.broadcasted_iota(jnp.int32, (CHUNK, LANES), 1) % HEAD
    strict = jw < trw
    incl = jw <= trw
    r2 = lax.broadcasted_iota(jnp.int32, (LANES, LANES), 0) // HEAD
    c2 = lax.broadcasted_iota(jnp.int32, (LANES, LANES), 1) // HEAD
    same_head = r2 == c2
    zb16 = jnp.zeros((), BF16)

    def bdiag(xw, swap=False):
        za = jnp.where(is_a, xw, jnp.zeros_like(xw))
        zb = jnp.where(is_a, jnp.zeros_like(xw), xw)
        return jnp.concatenate([zb, za] if swap else [za, zb], axis=0)

    streams = [(b, p) for b in range(n_b) for p in range(n_pair)]
    ns = range(len(streams))

    def part(name, n):
        b, p = streams[n]
        return pre[b][name][:, p * LANES:(p + 1) * LANES]

    xq = [jnp.concatenate([part("a_t", n), part("r_t", n)], axis=0) for n in ns]
    ycat = [jnp.concatenate(
        [jnp.where(is_a, jnp.concatenate([part("k_t", n), part("b_t", n)], axis=0), zb16),
         jnp.where(is_a, zb16, jnp.concatenate([part("b_t", n), part("k_t", n)], axis=0))],
        axis=0) for n in ns]
    sc = [_dot_nt(xq[n], ycat[n]) for n in ns]
    s0 = [st_ref[b, p] for b, p in streams]
    xs = [_dot_nt(xq[n], s0[n].astype(BF16)) for n in ns]
    top_lo = [sc[n][:CHUNK, :LANES] for n in ns]
    top_hi = [sc[n][:CHUNK, LANES:] for n in ns]
    bot_lo = [sc[n][CHUNK:, :LANES] for n in ns]
    bot_hi = [sc[n][CHUNK:, LANES:] for n in ns]
    a_ak = [jnp.where(strict, jnp.where(is_a, top_lo[n], top_hi[n]), 0.0).astype(BF16) for n in ns]
    a_ab = [jnp.where(strict, jnp.where(is_a, top_hi[n], top_lo[n]), 0.0) for n in ns]
    m_rk = [jnp.where(incl, jnp.where(is_a, bot_lo[n], bot_hi[n]), 0.0).astype(BF16) for n in ns]
    m_rb = [jnp.where(incl, jnp.where(is_a, bot_hi[n], bot_lo[n]), 0.0).astype(BF16) for n in ns]

    eye = (jw == trw).astype(F32)
    lvl1 = (trw // 2 == jw // 2) & strict
    tw = [eye + jnp.where(lvl1, a_ab[n], 0.0) for n in ns]
    s = 2
    while s < CHUNK:
        off = (trw // (2 * s) == jw // (2 * s)) & ((trw // s) % 2 == 1) & ((jw // s) % 2 == 0)
        tb = [t.astype(BF16) for t in tw]
        pm = [_dot(tb[n], bdiag(jnp.where(off, a_ab[n], 0.0).astype(BF16))) for n in ns]
        tw = [tw[n] + _dot(pm[n].astype(BF16), bdiag(tb[n])) for n in ns]
        s *= 2

    vbd = [bdiag(part("v_b", n)) for n in ns]
    gm = [xs[n][:CHUNK] + _dot(a_ak[n], vbd[n]) for n in ns]
    u = [_dot(tw[n].astype(BF16), bdiag(gm[n].astype(BF16), swap=True)).astype(BF16) for n in ns]
    ys = [xs[n][CHUNK:] + _dot(jnp.concatenate([m_rb[n], m_rk[n]], axis=1),
                               jnp.concatenate([bdiag(u[n], swap=True), vbd[n]], axis=0))
          for n in ns]
    for n in ns:
        b, p = streams[n]
        uv = jnp.concatenate([u[n], part("v_b", n)], axis=0)
        bk = jnp.concatenate([part("b_t", n), part("k_t", n)], axis=0)
        st_ref[b, p] = (s0[n] + jnp.where(same_head, _dot_tn(uv, bk), 0.0)) * part("w_end", n)

    inv_n = 1.0 / HEAD
    for b in range(n_b):
        y = jnp.concatenate(ys[b * n_pair:(b + 1) * n_pair], axis=1)
        q = pre[b]
        mean = _head_sums(y) * inv_n
        d = y - mean
        var = _head_sums(d * d) * inv_n
        yn = d * lax.rsqrt(var + GN_EPS) * lnw_ref[...] + lnb_ref[...]
        bonus = _head_sums(q["r"] * q["k"] * rk_ref[...]) * q["v"]
        o_ref[b] = ((yn + bonus) * (q["g"] * _sigmoid(q["g"]))).astype(BF16)


def _wkv(p3, tail3, mu, mu_t, w0, w_up, a0, a_up, k_k, k_a, r_k, ln_w, ln_b):
    B, L, n_main = p3.shape
    D = n_main // 4
    nc = L // CHUNK
    sub = CHUNK // 8
    n_b = WKV_BATCH if B % WKV_BATCH == 0 else 1

    def cur(s):
        return pl.BlockSpec((n_b, CHUNK, D), lambda b, c: (b, c, s))

    def halo(s):
        return pl.BlockSpec((n_b, 8, D), lambda b, c: (b, jnp.maximum(c * sub - 1, 0), s))

    def vec(n):
        return pl.BlockSpec((1, n), lambda b, c: (0, 0))

    lora = pl.BlockSpec((HEAD, D), lambda b, c: (0, 0))
    return pl.pallas_call(
        _wkv_kernel,
        grid=(B // n_b, nc),
        in_specs=[cur(0), cur(1), cur(2), cur(3), halo(0), halo(1), halo(2), halo(3),
                  pl.BlockSpec((n_b, CHUNK, LANES), lambda b, c: (b, c, 0)),
                  pl.BlockSpec((n_b, 8, LANES), lambda b, c: (b, jnp.maximum(c * sub - 1, 0), 0)),
                  vec(n_main), vec(LANES), vec(D), lora, vec(D), lora,
                  vec(D), vec(D), vec(D), vec(D), vec(D)],
        out_specs=pl.BlockSpec((n_b, CHUNK, D), lambda b, c: (b, c, 0)),
        out_shape=jax.ShapeDtypeStruct((B, L, D), BF16),
        scratch_shapes=[pltpu.VMEM((n_b, D // LANES, LANES, LANES), F32)],
        compiler_params=pltpu.CompilerParams(
            dimension_semantics=("arbitrary", "arbitrary"), vmem_limit_bytes=VMEM_LIMIT),
        name="wkv7",
    )(p3, p3, p3, p3, p3, p3, p3, p3, tail3, tail3,
      mu, mu_t, w0, w_up, a0, a_up, k_k, k_a, r_k, ln_w, ln_b)


def _fox_mixer(p, tail, B, L, b_f):
    D = p.shape[1] // 4
    n_heads = b_f.shape[0]
    b_row = jnp.pad(b_f, (0, LANES - n_heads)).reshape(1, LANES)
    kext, qrows = _fox_bias(tail.reshape(B, L, LANES), b_row, n_heads, D)
    p3 = p.reshape(B, L, 4 * D)
    v_t = p3[:, :, 2 * D:3 * D].reshape(B, L // BLK, BLK, D).transpose(0, 1, 3, 2)
    return _fox_attn(p3, kext, v_t, qrows).reshape(B * L, D)


def _rwkv_mixer(p, tail, B, L, mu, w0, w_up, a0, a_up, k_k, k_a, r_k, ln_w, ln_b):
    n_main = p.shape[1]
    row = lambda t: t.reshape(1, -1)
    z = _wkv(p.reshape(B, L, n_main), tail.reshape(B, L, LANES),
             row(mu[:n_main]), row(mu[n_main:]), row(w0), w_up.astype(BF16), row(a0),
             a_up.astype(BF16), row(k_k), row(k_a), row(r_k), row(ln_w), row(ln_b))
    return z.reshape(B * L, n_main // 4)


def kernel(x, meta_tokens, norm_pre, norm_post, fox_w_in, fox_b_f, fox_w_out, rwkv_w_in, rwkv_mu, rwkv_w0, rwkv_w_up, rwkv_a0, rwkv_a_up, rwkv_k_k, rwkv_k_a, rwkv_r_k, rwkv_ln_w, rwkv_ln_b, rwkv_w_out):
    B, S, D = x.shape
    depth = norm_pre.shape[0]
    n_heads = fox_b_f.shape[1]
    L0 = N_META + S
    L = -(-L0 // BLK) * BLK
    assert (B * L) % ROW_TILE == 0 and D % LANES == 0
    meta = jnp.broadcast_to(meta_tokens[None].astype(x.dtype), (B, N_META, D))
    h = jnp.concatenate([meta, x, jnp.zeros((B, L - L0, D), x.dtype)], axis=1).reshape(B * L, D)

    def in_proj_args(i):
        g = norm_pre[i].reshape(1, D)
        if i % 2 == 0:
            w = jnp.concatenate([fox_w_in[i // 2], jnp.zeros((D, LANES - n_heads), F32)], axis=1)
            return w.astype(BF16), g, dict(q_cols=D, q_scale=LOG2E * HEAD ** -0.5)
        return rwkv_w_in[i // 2].astype(BF16), g, {}

    w, g, qkw = in_proj_args(0)
    p, tail = _norm_proj(h, g, w, **qkw)
    for i in range(depth):
        j = i // 2
        if i % 2 == 0:
            z = _fox_mixer(p, tail, B, L, fox_b_f[j])
            w_out = fox_w_out[j]
        else:
            z = _rwkv_mixer(p, tail, B, L, rwkv_mu[j], rwkv_w0[j], rwkv_w_up[j], rwkv_a0[j],
                            rwkv_a_up[j], rwkv_k_k[j], rwkv_k_a[j], rwkv_r_k[j], rwkv_ln_w[j],
                            rwkv_ln_b[j])
            w_out = rwkv_w_out[j]
        g_post = norm_post[i].reshape(1, D)
        if i + 1 < depth:
            w, g, qkw = in_proj_args(i + 1)
            h, p, tail = _out_norm_proj(z, w_out.astype(BF16), h, g_post, g, w, **qkw)
        else:
            h = _out_proj(z, w_out.astype(BF16), h, g_post)
    return h.reshape(B, L, D)[:, N_META:L0]
```

```python
import functools
import math

import jax
import jax.numpy as jnp
from jax import lax
from jax.experimental import pallas as pl
from jax.experimental.pallas import tpu as pltpu

F32 = jnp.float32
BF16 = jnp.bfloat16

N_META = 16
HEAD = 64
LANES = 128
BLK = 128
ATTN_PAIRS = 8
ATTN_SUBGROUP = 8
KV_GROUP = 4
SUM_ROWS = 16
CHUNK = 64
WKV_BATCH = 4
ROW_TILE = 512
COL_TILE = 512
NORM_EPS = 1e-6
GN_EPS = 64e-5
DECAY_SCALE = math.exp(-0.5)
LOG2E = math.log2(math.e)
NEG = -1e30
VMEM_LIMIT = 48 * 1024 * 1024
VMEM_LIMIT_FUSED = 56 * 1024 * 1024


def _dot(a, b):
    return jnp.dot(a, b, preferred_element_type=F32)


def _dot_nt(a, b):
    return lax.dot_general(a, b, (((1,), (1,)), ((), ())), preferred_element_type=F32)


def _dot_tn(a, b):
    return lax.dot_general(a, b, (((0,), (0,)), ((), ())), preferred_element_type=F32)


def _split3(x):
    hi = x.astype(BF16)
    r1 = x - hi.astype(F32)
    mid = r1.astype(BF16)
    lo = (r1 - mid.astype(F32)).astype(BF16)
    return hi, mid, lo


def _sigmoid(x):
    return 0.5 * jnp.tanh(0.5 * x) + 0.5


def _project(x, g_ref, w_refs, outs, n_main, q_cols, q_scale):
    var = jnp.mean(x * x, axis=-1, keepdims=True)
    u = (x * lax.rsqrt(var + NORM_EPS) * g_ref[...]).astype(BF16)
    w_ref, (o_ref, t_ref) = w_refs[0], outs[:2]
    for j in range(0, n_main, COL_TILE):
        acc = _dot(u, w_ref[:, j:j + COL_TILE])
        if j < q_cols:
            acc = acc * q_scale
        o_ref[:, j:j + COL_TILE] = acc.astype(BF16)
    t_ref[...] = _dot(u, w_ref[:, n_main:])
    if len(w_refs) > 1:
        for j in range(0, w_refs[1].shape[0], COL_TILE):
            acc_t = _dot_nt(w_refs[1][j:j + COL_TILE, :], u).astype(BF16)
            for t in range(x.shape[0] // BLK):
                outs[2][t, j:j + COL_TILE, :] = acc_t[:, t * BLK:(t + 1) * BLK]


def _proj_specs(T, w, w_t):
    n_main = w.shape[1] - LANES
    whole = lambda a: pl.BlockSpec(a.shape, lambda i: (0,) * a.ndim)
    in_specs = [whole(w)]
    out_specs = [pl.BlockSpec((ROW_TILE, n_main), lambda i: (i, 0)),
                 pl.BlockSpec((ROW_TILE, LANES), lambda i: (i, 0))]
    shapes = [jax.ShapeDtypeStruct((T, n_main), BF16), jax.ShapeDtypeStruct((T, LANES), F32)]
    if w_t is not None:
        in_specs.append(whole(w_t))
        out_specs.append(pl.BlockSpec((ROW_TILE // BLK, w_t.shape[0], BLK), lambda i: (i, 0, 0)))
        shapes.append(jax.ShapeDtypeStruct((T // BLK, w_t.shape[0], BLK), BF16))
    return in_specs, out_specs, shapes


def _norm_proj_kernel(h_ref, g_ref, *refs, n_w, n_main, q_cols, q_scale):
    _project(h_ref[...], g_ref, refs[:n_w], refs[n_w:], n_main, q_cols, q_scale)


def _norm_proj(h, g, w, w_t=None, *, q_cols=0, q_scale=1.0):
    T, D = h.shape
    ws = [w] if w_t is None else [w, w_t]
    w_specs, out_specs, out_shape = _proj_specs(T, w, w_t)
    kern = functools.partial(_norm_proj_kernel, n_w=len(ws), n_main=w.shape[1] - LANES,
                             q_cols=q_cols, q_scale=q_scale)
    return pl.pallas_call(
        kern,
        grid=(T // ROW_TILE,),
        in_specs=[pl.BlockSpec((ROW_TILE, D), lambda i: (i, 0)),
                  pl.BlockSpec((1, D), lambda i: (0, 0))] + w_specs,
        out_specs=out_specs,
        out_shape=out_shape,
        compiler_params=pltpu.CompilerParams(
            dimension_semantics=("arbitrary",), vmem_limit_bytes=VMEM_LIMIT),
        name="norm_proj",
    )(h, g, *ws)


def _residual(z_ref, w_ref, h_ref, g_ref):
    m = _dot(z_ref[...], w_ref[...])
    var = jnp.mean(m * m, axis=-1, keepdims=True)
    return h_ref[...] + m * lax.rsqrt(var + NORM_EPS) * g_ref[...]


def _out_proj_kernel(z_ref, w_ref, h_ref, g_ref, o_ref):
    o_ref[...] = _residual(z_ref, w_ref, h_ref, g_ref)


def _out_norm_proj_kernel(z_ref, wo_ref, h_ref, gpost_ref, gpre_ref, *refs,
                          n_w, n_main, q_cols, q_scale):
    hn = _residual(z_ref, wo_ref, h_ref, gpost_ref)
    refs[n_w][...] = hn
    _project(hn, gpre_ref, refs[:n_w], refs[n_w + 1:], n_main, q_cols, q_scale)


def _out_norm_proj(z, w_out, h, g_post, g_pre, w, w_t=None, *, q_cols=0, q_scale=1.0):
    T, D = h.shape
    ws = [w] if w_t is None else [w, w_t]
    w_specs, out_specs, out_shape = _proj_specs(T, w, w_t)
    kern = functools.partial(_out_norm_proj_kernel, n_w=len(ws), n_main=w.shape[1] - LANES,
                             q_cols=q_cols, q_scale=q_scale)
    rows = lambda n: pl.BlockSpec((ROW_TILE, n), lambda i: (i, 0))
    whole = lambda a: pl.BlockSpec(a.shape, lambda i: (0, 0))
    return pl.pallas_call(
        kern,
        grid=(T // ROW_TILE,),
        in_specs=[rows(D), whole(w_out), rows(D), whole(g_post), whole(g_pre)] + w_specs,
        out_specs=[rows(D)] + out_specs,
        out_shape=[jax.ShapeDtypeStruct((T, D), F32)] + out_shape,
        compiler_params=pltpu.CompilerParams(
            dimension_semantics=("arbitrary",), vmem_limit_bytes=VMEM_LIMIT_FUSED),
        name="out_norm_proj",
    )(z, w_out, h, g_post, g_pre, *ws)


def _out_proj(z, w, h, g):
    T, D = h.shape
    return pl.pallas_call(
        _out_proj_kernel,
        grid=(T // ROW_TILE,),
        in_specs=[
            pl.BlockSpec((ROW_TILE, D), lambda i: (i, 0)),
            pl.BlockSpec((D, D), lambda i: (0, 0)),
            pl.BlockSpec((ROW_TILE, D), lambda i: (i, 0)),
            pl.BlockSpec((1, D), lambda i: (0, 0)),
        ],
        out_specs=pl.BlockSpec((ROW_TILE, D), lambda i: (i, 0)),
        out_shape=jax.ShapeDtypeStruct((T, D), F32),
        compiler_params=pltpu.CompilerParams(
            dimension_semantics=("arbitrary",), vmem_limit_bytes=VMEM_LIMIT),
        name="out_proj",
    )(z, w, h, g)


def _out_proj_final(z, w, h, g, B, L, S):
    D = h.shape[1]
    n_r = S // ROW_TILE
    rows_in = pl.BlockSpec((pl.Element(ROW_TILE), pl.Element(D)),
                           lambda b, r: (pl.multiple_of(b * L + N_META + r * ROW_TILE, 8), 0))
    return pl.pallas_call(
        _out_proj_kernel,
        grid=(B, n_r),
        in_specs=[rows_in, pl.BlockSpec((D, D), lambda b, r: (0, 0)), rows_in,
                  pl.BlockSpec((1, D), lambda b, r: (0, 0))],
        out_specs=pl.BlockSpec((ROW_TILE, D), lambda b, r: (b * n_r + r, 0)),
        out_shape=jax.ShapeDtypeStruct((B * S, D), F32),
        compiler_params=pltpu.CompilerParams(
            dimension_semantics=("arbitrary", "arbitrary"), vmem_limit_bytes=VMEM_LIMIT),
        name="out_proj_final",
    )(z, w, h, g)


def _fox_bias_kernel(t_ref, b_ref, x_ref, q_ref):
    L = t_ref.shape[1]
    width = x_ref.shape[2]
    n_heads = q_ref.shape[2]
    x = t_ref[0] + b_ref[...]
    lf = jnp.minimum(x, 0.0) - jnp.log(1.0 + jnp.exp(-jnp.abs(x)))
    row = lax.broadcasted_iota(jnp.int32, (BLK, BLK), 0)
    col = lax.broadcasted_iota(jnp.int32, (BLK, BLK), 1)
    tril = (col <= row).astype(BF16)
    carry = jnp.zeros((1, LANES), F32)
    blocks = []
    for j in range(L // BLK):
        hi, mid, lo = _split3(lf[j * BLK:(j + 1) * BLK])
        blk = _dot(tril, hi) + _dot(tril, mid) + _dot(tril, lo) + carry
        carry = blk[BLK - 1:BLK]
        blocks.append(blk)
    cum2 = jnp.concatenate(blocks, axis=0) * LOG2E

    head = lax.broadcasted_iota(jnp.int32, (LANES, width), 0)
    colw = lax.broadcasted_iota(jnp.int32, (LANES, width), 1)
    base = (head // 2) * LANES + (head % 2) * 3
    ext = jnp.zeros((L, width), F32)
    for o, term in enumerate(_split3(-cum2)):
        ext = ext + _dot(term, ((head < n_heads) & (colw == base + o)).astype(BF16))
    lane_w = lax.broadcasted_iota(jnp.int32, (1, width), 1) % LANES
    x_ref[0] = jnp.where((lane_w >= 6) & (lane_w < 9), 1.0, ext).astype(BF16)

    hrow = lax.broadcasted_iota(jnp.int32, (n_heads, LANES), 0)
    lane = lax.broadcasted_iota(jnp.int32, (n_heads, LANES), 1)
    slot = (hrow % 2) * 3
    own = ((lane >= slot) & (lane < slot + 3)).astype(F32)
    for j in range(L // BLK):
        diag = jnp.where(hrow == lane, blocks[j][0:1] * LOG2E, 0.0)
        rows = own
        for o, term in enumerate(_split3(diag)):
            rows = rows + _dot(term, (col == 6 + o).astype(BF16))
        q_ref[0, j] = rows


def _fox_bias(tail3, b_row, n_heads, width):
    B, L, _ = tail3.shape
    nq = L // BLK
    return pl.pallas_call(
        _fox_bias_kernel,
        grid=(B,),
        in_specs=[pl.BlockSpec((1, L, LANES), lambda b: (b, 0, 0)),
                  pl.BlockSpec((1, LANES), lambda b: (0, 0))],
        out_specs=[pl.BlockSpec((1, L, width), lambda b: (b, 0, 0)),
                   pl.BlockSpec((1, nq, n_heads, LANES), lambda b: (b, 0, 0, 0))],
        out_shape=[jax.ShapeDtypeStruct((B, L, width), BF16),
                   jax.ShapeDtypeStruct((B, nq, n_heads, LANES), F32)],
        compiler_params=pltpu.CompilerParams(
            dimension_semantics=("arbitrary",), vmem_limit_bytes=VMEM_LIMIT),
        name="fox_bias",
    )(tail3, b_row)


def _fox_attn_kernel(q_ref, k_ref, x_ref, vt_ref, g_ref, qx_ref, o_ref):
    i = pl.program_id(2)
    n_p = q_ref.shape[2] // LANES
    pairs = range(n_p)
    ps = [slice(p * LANES, (p + 1) * LANES) for p in pairs]
    lane = lax.broadcasted_iota(jnp.int32, (1, LANES), 1)
    is_a = lane < HEAD
    zq = jnp.zeros((), BF16)
    ones_rows = jnp.ones((SUM_ROWS, BLK), BF16)

    def ext_rows(h):
        return jnp.broadcast_to(qx_ref[0, 0, h:h + 1, :], (BLK, LANES)).astype(BF16)

    qx = []
    for p in pairs:
        q2 = q_ref[0, :, ps[p]]
        qx.append(jnp.concatenate(
            [jnp.concatenate([jnp.where(is_a, q2, zq), ext_rows(2 * p)], axis=1),
             jnp.concatenate([jnp.where(is_a, zq, q2), ext_rows(2 * p + 1)], axis=1)], axis=0))

    key_i = lax.broadcasted_iota(jnp.int32, (BLK, 2 * BLK), 0)
    qry_i = lax.broadcasted_iota(jnp.int32, (BLK, 2 * BLK), 1) % BLK
    keep = key_i <= qry_i

    def step(j, carry, nb=1):
        m, acc = carry
        off = pl.multiple_of(j * BLK, BLK)
        unmasked = jnp.logical_or(keep, j + (nb - 1) != i)
        m, acc = list(m), list(acc)
        n_sub = ATTN_SUBGROUP if nb > 1 else n_p
        for g0 in range(0, n_p, n_sub):
            sub = range(g0, min(g0 + n_sub, n_p))
            st = {}
            for p in sub:
                kx = jnp.concatenate([k_ref[0, pl.ds(off, nb * BLK), ps[p]],
                                      x_ref[0, pl.ds(off, nb * BLK), ps[p]]], axis=1)
                s = _dot_nt(kx, qx[p])
                last = jnp.where(unmasked, s[(nb - 1) * BLK:], NEG)
                st[p] = last if nb == 1 else jnp.concatenate([s[:(nb - 1) * BLK], last], axis=0)
            for p in sub:
                mn = jnp.maximum(m[p], jnp.max(st[p], axis=0, keepdims=True))
                alpha = jnp.exp2(m[p] - mn)
                pr = jnp.exp2(st[p] - mn).astype(BF16)
                vt = jnp.concatenate(
                    [jnp.concatenate([vt_ref[0, j + t, ps[p], :], ones_rows], axis=0)
                     for t in range(nb)], axis=1)
                acc[p] = acc[p] * alpha + _dot(vt, pr)
                m[p] = mn
        return tuple(m), tuple(acc)

    n_grp = (i + 1) // KV_GROUP
    carry = (tuple(jnp.full((1, 2 * BLK), NEG, F32) for _ in pairs),
             tuple(jnp.zeros((LANES + SUM_ROWS, 2 * BLK), F32) for _ in pairs))
    carry = lax.fori_loop(0, n_grp, lambda jj, c: step(jj * KV_GROUP, c, KV_GROUP), carry)
    _, acc = lax.fori_loop(n_grp * KV_GROUP, i + 1, step, carry)
    for p in pairs:
        o_t = acc[p][:LANES] / acc[p][LANES:LANES + 1]
        o = jnp.concatenate([o_t[:HEAD, :BLK], o_t[HEAD:LANES, BLK:]], axis=0).T
        g = g_ref[0, :, ps[p]].astype(F32)
        o_ref[0, :, ps[p]] = (o * (g * _sigmoid(g))).astype(BF16)


def _fox_attn(p3, kext, v_t, qrows):
    B, L, _ = p3.shape
    n_pair = (p3.shape[2] // 3) // LANES
    nq = L // BLK
    n_p = ATTN_PAIRS
    width = n_p * LANES
    n_grp = n_pair // n_p
    assert n_grp == 1 and qrows.shape[2] == 2 * n_pair
    return pl.pallas_call(
        _fox_attn_kernel,
        grid=(B, n_grp, nq),
        in_specs=[
            pl.BlockSpec((1, BLK, width), lambda b, p, i: (b, i, p)),
            pl.BlockSpec((1, L, width), lambda b, p, i: (b, 0, n_grp + p)),
            pl.BlockSpec((1, L, width), lambda b, p, i: (b, 0, p)),
            pl.BlockSpec((1, nq, width, BLK), lambda b, p, i: (b, 0, p, 0)),
            pl.BlockSpec((1, BLK, width), lambda b, p, i: (b, i, 2 * n_grp + p)),
            pl.BlockSpec((1, 1, 2 * n_pair, LANES), lambda b, p, i: (b, i, 0, 0)),
        ],
        out_specs=pl.BlockSpec((1, BLK, width), lambda b, p, i: (b, i, p)),
        out_shape=jax.ShapeDtypeStruct((B, L, n_pair * LANES), BF16),
        compiler_params=pltpu.CompilerParams(
            dimension_semantics=("arbitrary", "arbitrary", "arbitrary"), vmem_limit_bytes=VMEM_LIMIT),
        name="fox_attn",
    )(p3, p3, kext, v_t, p3, qrows)


def _head_sums(x):
    row = lax.broadcasted_iota(jnp.int32, (LANES, LANES), 0)
    col = lax.broadcasted_iota(jnp.int32, (LANES, LANES), 1)
    ones = ((row // HEAD) == (col // HEAD)).astype(BF16)
    parts = []
    for g in range(x.shape[1] // LANES):
        parts.append(_dot(x[:, g * LANES:(g + 1) * LANES].astype(BF16), ones))
    return jnp.concatenate(parts, axis=1)


def _wkv_kernel(r_ref, k_ref, v_ref, g_ref, rh_ref, kh_ref, vh_ref, gh_ref, t_ref, th_ref,
                mu_ref, mut_ref, w0_ref, wup_ref, a0_ref, aup_ref, kk_ref, ka_ref, rk_ref,
                lnw_ref, lnb_ref, o_ref, st_ref, *, n_real):
    c = pl.program_id(1)

    @pl.when(c == 0)
    def _():
        st_ref[...] = jnp.zeros_like(st_ref)

    @pl.when(c >= n_real)
    def _():
        o_ref[...] = jnp.zeros_like(o_ref)

    @pl.when(c < n_real)
    def _():
        _wkv_chunk(r_ref, k_ref, v_ref, g_ref, rh_ref, kh_ref, vh_ref, gh_ref, t_ref, th_ref,
                   mu_ref, mut_ref, w0_ref, wup_ref, a0_ref, aup_ref, kk_ref, ka_ref, rk_ref,
                   lnw_ref, lnb_ref, o_ref, st_ref, c)


def _wkv_chunk(r_ref, k_ref, v_ref, g_ref, rh_ref, kh_ref, vh_ref, gh_ref, t_ref, th_ref,
               mu_ref, mut_ref, w0_ref, wup_ref, a0_ref, aup_ref, kk_ref, ka_ref, rk_ref,
               lnw_ref, lnb_ref, o_ref, st_ref, c):
    n_b = r_ref.shape[0]
    D = r_ref.shape[2]
    n_pair = D // LANES
    not_first = c != 0
    row = lax.broadcasted_iota(jnp.int32, (CHUNK, CHUNK), 0)
    col = lax.broadcasted_iota(jnp.int32, (CHUNK, CHUNK), 1)
    sub_diag = (col + 1 == row).astype(BF16)
    tril = (col <= row).astype(BF16)
    top0 = (lax.broadcasted_iota(jnp.int32, (8, 1), 0) == 0) & not_first
    trow0 = lax.broadcasted_iota(jnp.int32, (CHUNK, 1), 0) == 0

    def lerp(cur, sh, prev, mu):
        sh = jnp.concatenate([sh[0:8] + jnp.where(top0, prev, 0.0), sh[8:]], axis=0)
        return cur + (sh - cur) * mu

    def shifted(cur, prev, mu):
        return lerp(cur.astype(F32), _dot(sub_diag, cur), prev.astype(F32), mu)

    def prep(b):
        r = shifted(r_ref[b], rh_ref[b, 7:8, :], mu_ref[:, 0:D])
        k = shifted(k_ref[b], kh_ref[b, 7:8, :], mu_ref[:, D:2 * D])
        v = shifted(v_ref[b], vh_ref[b, 7:8, :], mu_ref[:, 2 * D:3 * D])
        g = shifted(g_ref[b], gh_ref[b, 7:8, :], mu_ref[:, 3 * D:4 * D])
        tcur = t_ref[b]
        tl = lerp(tcur, jnp.where(trow0, 0.0, pltpu.roll(tcur, 1, axis=0)), th_ref[b, 7:8, :],
                  mut_ref[...])
        w_log = w0_ref[...] + _dot(jnp.tanh(tl[:, :HEAD]).astype(BF16), wup_ref[...])
        lw = (-0.5 * DECAY_SCALE * LOG2E) * jnp.tanh(0.5 * w_log) - 0.5 * DECAY_SCALE * LOG2E
        a = _sigmoid(a0_ref[...] + _dot(tl[:, HEAD:].astype(BF16), aup_ref[...]))
        kk = k * kk_ref[...]
        kk = kk * lax.rsqrt(jnp.maximum(_head_sums(kk * kk), 1e-24))
        k = k * (1.0 + (a - 1.0) * ka_ref[...])
        hi, mid, lo = _split3(lw)
        cl = _dot(tril, hi) + _dot(tril, mid) + _dot(tril, lo)
        e_neg = jnp.exp2(-cl)
        return dict(
            r=r, k=k, v=v, g=g,
            r_t=(r * jnp.exp2(cl)).astype(BF16),
            a_t=(-kk * jnp.exp2(cl - lw)).astype(BF16),
            b_t=(kk * a * e_neg).astype(BF16),
            k_t=(k * e_neg).astype(BF16),
            w_end=jnp.exp2(cl[CHUNK - 1:CHUNK, :]),
            v_b=v.astype(BF16))

    pre = [prep(b) for b in range(n_b)]

    lane = lax.broadcasted_iota(jnp.int32, (1, LANES), 1)
    is_a = lane < HEAD
    trw = lax.broadcasted_iota(jnp.int32, (CHUNK, LANES), 0)
    jw = lax.broadcasted_iota(jnp.int32, (CHUNK, LANES), 1) % HEAD
    strict = jw < trw
    incl = jw <= trw
    r2 = lax.broadcasted_iota(jnp.int32, (LANES, LANES), 0) // HEAD
    c2 = lax.broadcasted_iota(jnp.int32, (LANES, LANES), 1) // HEAD
    same_head = r2 == c2
    zb16 = jnp.zeros((), BF16)

    def bdiag(xw, swap=False):
        za = jnp.where(is_a, xw, jnp.zeros_like(xw))
        zb = jnp.where(is_a, jnp.zeros_like(xw), xw)
        return jnp.concatenate([zb, za] if swap else [za, zb], axis=0)

    streams = [(b, p) for b in range(n_b) for p in range(n_pair)]
    ns = range(len(streams))

    def part(name, n):
        b, p = streams[n]
        return pre[b][name][:, p * LANES:(p + 1) * LANES]

    xq = [jnp.concatenate([part("a_t", n), part("r_t", n)], axis=0) for n in ns]
    ycat = [jnp.concatenate(
        [jnp.where(is_a, jnp.concatenate([part("k_t", n), part("b_t", n)], axis=0), zb16),
         jnp.where(is_a, zb16, jnp.concatenate([part("b_t", n), part("k_t", n)], axis=0))],
        axis=0) for n in ns]
    sc = [_dot_nt(xq[n], ycat[n]) for n in ns]
    s0 = [st_ref[b, p] for b, p in streams]
    xs = [_dot_nt(xq[n], s0[n].astype(BF16)) for n in ns]
    top_lo = [sc[n][:CHUNK, :LANES] for n in ns]
    top_hi = [sc[n][:CHUNK, LANES:] for n in ns]
    bot_lo = [sc[n][CHUNK:, :LANES] for n in ns]
    bot_hi = [sc[n][CHUNK:, LANES:] for n in ns]
    a_ak = [jnp.where(strict, jnp.where(is_a, top_lo[n], top_hi[n]), 0.0).astype(BF16) for n in ns]
    a_ab = [jnp.where(strict, jnp.where(is_a, top_hi[n], top_lo[n]), 0.0) for n in ns]
    m_rk = [jnp.where(incl, jnp.where(is_a, bot_lo[n], bot_hi[n]), 0.0).astype(BF16) for n in ns]
    m_rb = [jnp.where(incl, jnp.where(is_a, bot_hi[n], bot_lo[n]), 0.0).astype(BF16) for n in ns]

    eye = (jw == trw).astype(F32)
    lvl1 = (trw // 2 == jw // 2) & strict
    tw = [eye + jnp.where(lvl1, a_ab[n], 0.0) for n in ns]
    s = 2
    while s < CHUNK:
        off = (trw // (2 * s) == jw // (2 * s)) & ((trw // s) % 2 == 1) & ((jw // s) % 2 == 0)
        tb = [t.astype(BF16) for t in tw]
        pm = [_dot(tb[n], bdiag(jnp.where(off, a_ab[n], 0.0).astype(BF16))) for n in ns]
        tw = [tw[n] + _dot(pm[n].astype(BF16), bdiag(tb[n])) for n in ns]
        s *= 2

    vbd = [bdiag(part("v_b", n)) for n in ns]
    gm = [xs[n][:CHUNK] + _dot(a_ak[n], vbd[n]) for n in ns]
    u = [_dot(tw[n].astype(BF16), bdiag(gm[n].astype(BF16), swap=True)).astype(BF16) for n in ns]
    ys = [xs[n][CHUNK:] + _dot(jnp.concatenate([m_rb[n], m_rk[n]], axis=1),
                               jnp.concatenate([bdiag(u[n], swap=True), vbd[n]], axis=0))
          for n in ns]
    for n in ns:
        b, p = streams[n]
        uv = jnp.concatenate([u[n], part("v_b", n)], axis=0)
        bk = jnp.concatenate([part("b_t", n), part("k_t", n)], axis=0)
        st_ref[b, p] = (s0[n] + jnp.where(same_head, _dot_tn(uv, bk), 0.0)) * part("w_end", n)

    inv_n = 1.0 / HEAD
    for b in range(n_b):
        y = jnp.concatenate(ys[b * n_pair:(b + 1) * n_pair], axis=1)
        q = pre[b]
        mean = _head_sums(y) * inv_n
        d = y - mean
        var = _head_sums(d * d) * inv_n
        yn = d * lax.rsqrt(var + GN_EPS) * lnw_ref[...] + lnb_ref[...]
        bonus = _head_sums(q["r"] * q["k"] * rk_ref[...]) * q["v"]
        o_ref[b] = ((yn + bonus) * (q["g"] * _sigmoid(q["g"]))).astype(BF16)


def _wkv(p3, tail3, n_real, mu, mu_t, w0, w_up, a0, a_up, k_k, k_a, r_k, ln_w, ln_b):
    B, L, n_main = p3.shape
    D = n_main // 4
    nc = L // CHUNK
    sub = CHUNK // 8
    n_b = WKV_BATCH if B % WKV_BATCH == 0 else 1

    def cur(s):
        return pl.BlockSpec((n_b, CHUNK, D), lambda b, c: (b, c, s))

    def halo(s):
        return pl.BlockSpec((n_b, 8, D), lambda b, c: (b, jnp.maximum(c * sub - 1, 0), s))

    def vec(n):
        return pl.BlockSpec((1, n), lambda b, c: (0, 0))

    lora = pl.BlockSpec((HEAD, D), lambda b, c: (0, 0))
    return pl.pallas_call(
        functools.partial(_wkv_kernel, n_real=n_real),
        grid=(B // n_b, nc),
        in_specs=[cur(0), cur(1), cur(2), cur(3), halo(0), halo(1), halo(2), halo(3),
                  pl.BlockSpec((n_b, CHUNK, LANES), lambda b, c: (b, c, 0)),
                  pl.BlockSpec((n_b, 8, LANES), lambda b, c: (b, jnp.maximum(c * sub - 1, 0), 0)),
                  vec(n_main), vec(LANES), vec(D), lora, vec(D), lora,
                  vec(D), vec(D), vec(D), vec(D), vec(D)],
        out_specs=pl.BlockSpec((n_b, CHUNK, D), lambda b, c: (b, c, 0)),
        out_shape=jax.ShapeDtypeStruct((B, L, D), BF16),
        scratch_shapes=[pltpu.VMEM((n_b, D // LANES, LANES, LANES), F32)],
        compiler_params=pltpu.CompilerParams(
            dimension_semantics=("arbitrary", "arbitrary"), vmem_limit_bytes=VMEM_LIMIT),
        name="wkv7",
    )(p3, p3, p3, p3, p3, p3, p3, p3, tail3, tail3,
      mu, mu_t, w0, w_up, a0, a_up, k_k, k_a, r_k, ln_w, ln_b)


def _fox_mixer(p, tail, v_t, B, L, b_f):
    D = p.shape[1] // 3
    n_heads = b_f.shape[0]
    b_row = jnp.pad(b_f, (0, LANES - n_heads)).reshape(1, LANES)
    kext, qrows = _fox_bias(tail.reshape(B, L, LANES), b_row, n_heads, D)
    return _fox_attn(p.reshape(B, L, 3 * D), kext, v_t.reshape(B, L // BLK, D, BLK),
                     qrows).reshape(B * L, D)


def _rwkv_mixer(p, tail, B, L, L0, mu, w0, w_up, a0, a_up, k_k, k_a, r_k, ln_w, ln_b):
    n_main = p.shape[1]
    row = lambda t: t.reshape(1, -1)
    z = _wkv(p.reshape(B, L, n_main), tail.reshape(B, L, LANES), -(-L0 // CHUNK),
             row(mu[:n_main]), row(mu[n_main:]), row(w0), w_up.astype(BF16), row(a0),
             a_up.astype(BF16), row(k_k), row(k_a), row(r_k), row(ln_w), row(ln_b))
    return z.reshape(B * L, n_main // 4)


def kernel(x, meta_tokens, norm_pre, norm_post, fox_w_in, fox_b_f, fox_w_out, rwkv_w_in, rwkv_mu, rwkv_w0, rwkv_w_up, rwkv_a0, rwkv_a_up, rwkv_k_k, rwkv_k_a, rwkv_r_k, rwkv_ln_w, rwkv_ln_b, rwkv_w_out):
    B, S, D = x.shape
    depth = norm_pre.shape[0]
    n_heads = fox_b_f.shape[1]
    L0 = N_META + S
    L = -(-L0 // BLK) * BLK
    assert (B * L) % ROW_TILE == 0 and D % LANES == 0
    meta = jnp.broadcast_to(meta_tokens[None].astype(x.dtype), (B, N_META, D))
    h = jnp.concatenate([meta, x, jnp.zeros((B, L - L0, D), x.dtype)], axis=1).reshape(B * L, D)

    def in_proj_args(i):
        g = norm_pre[i].reshape(1, D)
        if i % 2 == 0:
            wf = fox_w_in[i // 2]
            w = jnp.concatenate([wf[:, :2 * D], wf[:, 3 * D:], jnp.zeros((D, LANES - n_heads), F32)],
                                axis=1)
            return ((w.astype(BF16), wf[:, 2 * D:3 * D].T.astype(BF16)), g,
                    dict(q_cols=D, q_scale=LOG2E * HEAD ** -0.5))
        return (rwkv_w_in[i // 2].astype(BF16),), g, {}

    ws, g, qkw = in_proj_args(0)
    p, tail, *v_t = _norm_proj(h, g, *ws, **qkw)
    for i in range(depth):
        j = i // 2
        if i % 2 == 0:
            z = _fox_mixer(p, tail, v_t[0], B, L, fox_b_f[j])
            w_out = fox_w_out[j]
        else:
            z = _rwkv_mixer(p, tail, B, L, L0, rwkv_mu[j], rwkv_w0[j], rwkv_w_up[j], rwkv_a0[j],
                            rwkv_a_up[j], rwkv_k_k[j], rwkv_k_a[j], rwkv_r_k[j], rwkv_ln_w[j],
                            rwkv_ln_b[j])
            w_out = rwkv_w_out[j]
        g_post = norm_post[i].reshape(1, D)
        if i + 1 < depth:
            ws, g, qkw = in_proj_args(i + 1)
            h, p, tail, *v_t = _out_norm_proj(z, w_out.astype(BF16), h, g_post, g, *ws, **qkw)
        elif S % ROW_TILE == 0:
            return _out_proj_final(z, w_out.astype(BF16), h, g_post, B, L, S).reshape(B, S, D)
        else:
            h = _out_proj(z, w_out.astype(BF16), h, g_post)
    return h.reshape(B, L, D)[:, N_META:L0]
```

```python
import functools
import math

import jax
import jax.numpy as jnp
from jax import lax
from jax.experimental import pallas as pl
from jax.experimental.pallas import tpu as pltpu

F32 = jnp.float32
BF16 = jnp.bfloat16

N_META = 16
HEAD = 64
LANES = 128
BLK = 128
ATTN_PAIRS = 8
ATTN_SUBGROUP = 8
KV_GROUP = 4
SUM_ROWS = 16
CHUNK = 64
WKV_BATCH = 4
ROW_TILE = 512
COL_TILE = 512
NORM_EPS = 1e-6
GN_EPS = 64e-5
DECAY_SCALE = math.exp(-0.5)
LOG2E = math.log2(math.e)
NEG = -1e30
VMEM_LIMIT = 48 * 1024 * 1024
VMEM_LIMIT_FUSED = 56 * 1024 * 1024


def _dot(a, b):
    return jnp.dot(a, b, preferred_element_type=F32)


def _dot_nt(a, b):
    return lax.dot_general(a, b, (((1,), (1,)), ((), ())), preferred_element_type=F32)


def _dot_tn(a, b):
    return lax.dot_general(a, b, (((0,), (0,)), ((), ())), preferred_element_type=F32)


def _split3(x):
    hi = x.astype(BF16)
    r1 = x - hi.astype(F32)
    mid = r1.astype(BF16)
    lo = (r1 - mid.astype(F32)).astype(BF16)
    return hi, mid, lo


def _sigmoid(x):
    return 0.5 * jnp.tanh(0.5 * x) + 0.5


def _project(x, g_ref, w_refs, outs, n_main, q_cols, q_scale):
    var = jnp.mean(x * x, axis=-1, keepdims=True)
    u = (x * lax.rsqrt(var + NORM_EPS) * g_ref[...]).astype(BF16)
    w_ref, (o_ref, t_ref) = w_refs[0], outs[:2]
    for j in range(0, n_main, COL_TILE):
        acc = _dot(u, w_ref[:, j:j + COL_TILE])
        if j < q_cols:
            acc = acc * q_scale
        o_ref[:, j:j + COL_TILE] = acc.astype(BF16)
    t_ref[...] = _dot(u, w_ref[:, n_main:])
    if len(w_refs) > 1:
        for j in range(0, w_refs[1].shape[0], COL_TILE):
            acc_t = _dot_nt(w_refs[1][j:j + COL_TILE, :], u).astype(BF16)
            for t in range(x.shape[0] // BLK):
                outs[2][t, j:j + COL_TILE, :] = acc_t[:, t * BLK:(t + 1) * BLK]


def _proj_specs(T, w, w_t):
    n_main = w.shape[1] - LANES
    whole = lambda a: pl.BlockSpec(a.shape, lambda i: (0,) * a.ndim)
    in_specs = [whole(w)]
    out_specs = [pl.BlockSpec((ROW_TILE, n_main), lambda i: (i, 0)),
                 pl.BlockSpec((ROW_TILE, LANES), lambda i: (i, 0))]
    shapes = [jax.ShapeDtypeStruct((T, n_main), BF16), jax.ShapeDtypeStruct((T, LANES), F32)]
    if w_t is not None:
        in_specs.append(whole(w_t))
        out_specs.append(pl.BlockSpec((ROW_TILE // BLK, w_t.shape[0], BLK), lambda i: (i, 0, 0)))
        shapes.append(jax.ShapeDtypeStruct((T // BLK, w_t.shape[0], BLK), BF16))
    return in_specs, out_specs, shapes


def _norm_proj_kernel(h_ref, g_ref, *refs, n_w, n_main, q_cols, q_scale):
    _project(h_ref[...], g_ref, refs[:n_w], refs[n_w:], n_main, q_cols, q_scale)


def _norm_proj(h, g, w, w_t=None, *, q_cols=0, q_scale=1.0):
    T, D = h.shape
    ws = [w] if w_t is None else [w, w_t]
    w_specs, out_specs, out_shape = _proj_specs(T, w, w_t)
    kern = functools.partial(_norm_proj_kernel, n_w=len(ws), n_main=w.shape[1] - LANES,
                             q_cols=q_cols, q_scale=q_scale)
    return pl.pallas_call(
        kern,
        grid=(T // ROW_TILE,),
        in_specs=[pl.BlockSpec((ROW_TILE, D), lambda i: (i, 0)),
                  pl.BlockSpec((1, D), lambda i: (0, 0))] + w_specs,
        out_specs=out_specs,
        out_shape=out_shape,
        compiler_params=pltpu.CompilerParams(
            dimension_semantics=("arbitrary",), vmem_limit_bytes=VMEM_LIMIT),
        name="norm_proj",
    )(h, g, *ws)


def _residual(z_ref, w_ref, h_ref, g_ref):
    m = _dot(z_ref[...], w_ref[...])
    var = jnp.mean(m * m, axis=-1, keepdims=True)
    return h_ref[...] + m * lax.rsqrt(var + NORM_EPS) * g_ref[...]


def _out_proj_kernel(z_ref, w_ref, h_ref, g_ref, o_ref):
    o_ref[...] = _residual(z_ref, w_ref, h_ref, g_ref)


def _out_norm_proj_kernel(z_ref, wo_ref, h_ref, gpost_ref, gpre_ref, *refs,
                          n_w, n_main, q_cols, q_scale):
    hn = _residual(z_ref, wo_ref, h_ref, gpost_ref)
    refs[n_w][...] = hn
    _project(hn, gpre_ref, refs[:n_w], refs[n_w + 1:], n_main, q_cols, q_scale)


def _out_norm_proj(z, w_out, h, g_post, g_pre, w, w_t=None, *, q_cols=0, q_scale=1.0):
    T, D = h.shape
    ws = [w] if w_t is None else [w, w_t]
    w_specs, out_specs, out_shape = _proj_specs(T, w, w_t)
    kern = functools.partial(_out_norm_proj_kernel, n_w=len(ws), n_main=w.shape[1] - LANES,
                             q_cols=q_cols, q_scale=q_scale)
    rows = lambda n: pl.BlockSpec((ROW_TILE, n), lambda i: (i, 0))
    whole = lambda a: pl.BlockSpec(a.shape, lambda i: (0, 0))
    return pl.pallas_call(
        kern,
        grid=(T // ROW_TILE,),
        in_specs=[rows(D), whole(w_out), rows(D), whole(g_post), whole(g_pre)] + w_specs,
        out_specs=[rows(D)] + out_specs,
        out_shape=[jax.ShapeDtypeStruct((T, D), F32)] + out_shape,
        compiler_params=pltpu.CompilerParams(
            dimension_semantics=("arbitrary",), vmem_limit_bytes=VMEM_LIMIT_FUSED),
        name="out_norm_proj",
    )(z, w_out, h, g_post, g_pre, *ws)


def _out_proj(z, w, h, g):
    T, D = h.shape
    return pl.pallas_call(
        _out_proj_kernel,
        grid=(T // ROW_TILE,),
        in_specs=[
            pl.BlockSpec((ROW_TILE, D), lambda i: (i, 0)),
            pl.BlockSpec((D, D), lambda i: (0, 0)),
            pl.BlockSpec((ROW_TILE, D), lambda i: (i, 0)),
            pl.BlockSpec((1, D), lambda i: (0, 0)),
        ],
        out_specs=pl.BlockSpec((ROW_TILE, D), lambda i: (i, 0)),
        out_shape=jax.ShapeDtypeStruct((T, D), F32),
        compiler_params=pltpu.CompilerParams(
            dimension_semantics=("arbitrary",), vmem_limit_bytes=VMEM_LIMIT),
        name="out_proj",
    )(z, w, h, g)


def _out_proj_final(z, w, h, g, B, L, S):
    D = h.shape[1]
    n_r = S // ROW_TILE
    rows_in = pl.BlockSpec((pl.Element(ROW_TILE), pl.Element(D)),
                           lambda b, r: (pl.multiple_of(b * L + N_META + r * ROW_TILE, 8), 0))
    return pl.pallas_call(
        _out_proj_kernel,
        grid=(B, n_r),
        in_specs=[rows_in, pl.BlockSpec((D, D), lambda b, r: (0, 0)), rows_in,
                  pl.BlockSpec((1, D), lambda b, r: (0, 0))],
        out_specs=pl.BlockSpec((ROW_TILE, D), lambda b, r: (b * n_r + r, 0)),
        out_shape=jax.ShapeDtypeStruct((B * S, D), F32),
        compiler_params=pltpu.CompilerParams(
            dimension_semantics=("arbitrary", "arbitrary"), vmem_limit_bytes=VMEM_LIMIT),
        name="out_proj_final",
    )(z, w, h, g)


def _fox_bias_kernel(t_ref, b_ref, x_ref, q_ref):
    L = t_ref.shape[1]
    width = x_ref.shape[2]
    n_heads = q_ref.shape[2]
    x = t_ref[0] + b_ref[...]
    lf = jnp.minimum(x, 0.0) - jnp.log(1.0 + jnp.exp(-jnp.abs(x)))
    row = lax.broadcasted_iota(jnp.int32, (BLK, BLK), 0)
    col = lax.broadcasted_iota(jnp.int32, (BLK, BLK), 1)
    tril = (col <= row).astype(BF16)
    carry = jnp.zeros((1, LANES), F32)
    blocks = []
    for j in range(L // BLK):
        hi, mid, lo = _split3(lf[j * BLK:(j + 1) * BLK])
        blk = _dot(tril, hi) + _dot(tril, mid) + _dot(tril, lo) + carry
        carry = blk[BLK - 1:BLK]
        blocks.append(blk)
    cum2 = jnp.concatenate(blocks, axis=0) * LOG2E

    head = lax.broadcasted_iota(jnp.int32, (LANES, width), 0)
    colw = lax.broadcasted_iota(jnp.int32, (LANES, width), 1)
    base = (head // 2) * LANES + (head % 2) * 3
    ext = jnp.zeros((L, width), F32)
    for o, term in enumerate(_split3(-cum2)):
        ext = ext + _dot(term, ((head < n_heads) & (colw == base + o)).astype(BF16))
    lane_w = lax.broadcasted_iota(jnp.int32, (1, width), 1) % LANES
    x_ref[0] = jnp.where((lane_w >= 6) & (lane_w < 9), 1.0, ext).astype(BF16)

    hrow = lax.broadcasted_iota(jnp.int32, (n_heads, LANES), 0)
    lane = lax.broadcasted_iota(jnp.int32, (n_heads, LANES), 1)
    slot = (hrow % 2) * 3
    own = ((lane >= slot) & (lane < slot + 3)).astype(F32)
    for j in range(L // BLK):
        diag = jnp.where(hrow == lane, blocks[j][0:1] * LOG2E, 0.0)
        rows = own
        for o, term in enumerate(_split3(diag)):
            rows = rows + _dot(term, (col == 6 + o).astype(BF16))
        q_ref[0, j] = rows


def _fox_bias(tail3, b_row, n_heads, width):
    B, L, _ = tail3.shape
    nq = L // BLK
    return pl.pallas_call(
        _fox_bias_kernel,
        grid=(B,),
        in_specs=[pl.BlockSpec((1, L, LANES), lambda b: (b, 0, 0)),
                  pl.BlockSpec((1, LANES), lambda b: (0, 0))],
        out_specs=[pl.BlockSpec((1, L, width), lambda b: (b, 0, 0)),
                   pl.BlockSpec((1, nq, n_heads, LANES), lambda b: (b, 0, 0, 0))],
        out_shape=[jax.ShapeDtypeStruct((B, L, width), BF16),
                   jax.ShapeDtypeStruct((B, nq, n_heads, LANES), F32)],
        compiler_params=pltpu.CompilerParams(
            dimension_semantics=("arbitrary",), vmem_limit_bytes=VMEM_LIMIT),
        name="fox_bias",
    )(tail3, b_row)


def _fox_attn_kernel(q_ref, k_ref, x_ref, vt_ref, g_ref, qx_ref, o_ref):
    i = pl.program_id(2)
    n_p = q_ref.shape[2] // LANES
    pairs = range(n_p)
    ps = [slice(p * LANES, (p + 1) * LANES) for p in pairs]
    lane = lax.broadcasted_iota(jnp.int32, (1, LANES), 1)
    is_a = lane < HEAD
    zq = jnp.zeros((), BF16)
    ones_rows = jnp.ones((SUM_ROWS, BLK), BF16)

    def ext_rows(h):
        return jnp.broadcast_to(qx_ref[0, 0, h:h + 1, :], (BLK, LANES)).astype(BF16)

    qx = []
    for p in pairs:
        q2 = q_ref[0, :, ps[p]]
        qx.append(jnp.concatenate(
            [jnp.concatenate([jnp.where(is_a, q2, zq), ext_rows(2 * p)], axis=1),
             jnp.concatenate([jnp.where(is_a, zq, q2), ext_rows(2 * p + 1)], axis=1)], axis=0))

    key_i = lax.broadcasted_iota(jnp.int32, (BLK, 2 * BLK), 0)
    qry_i = lax.broadcasted_iota(jnp.int32, (BLK, 2 * BLK), 1) % BLK
    keep = key_i <= qry_i

    def step(j, carry, nb=1):
        m, acc = carry
        off = pl.multiple_of(j * BLK, BLK)
        unmasked = jnp.logical_or(keep, j + (nb - 1) != i)
        m, acc = list(m), list(acc)
        n_sub = ATTN_SUBGROUP if nb > 1 else n_p
        for g0 in range(0, n_p, n_sub):
            sub = range(g0, min(g0 + n_sub, n_p))
            st = {}
            for p in sub:
                kx = jnp.concatenate([k_ref[0, pl.ds(off, nb * BLK), ps[p]],
                                      x_ref[0, pl.ds(off, nb * BLK), ps[p]]], axis=1)
                s = _dot_nt(kx, qx[p])
                last = jnp.where(unmasked, s[(nb - 1) * BLK:], NEG)
                st[p] = last if nb == 1 else jnp.concatenate([s[:(nb - 1) * BLK], last], axis=0)
            for p in sub:
                mn = jnp.maximum(m[p], jnp.max(st[p], axis=0, keepdims=True))
                alpha = jnp.exp2(m[p] - mn)
                pr = jnp.exp2(st[p] - mn).astype(BF16)
                vt = jnp.concatenate(
                    [jnp.concatenate([vt_ref[0, j + t, ps[p], :], ones_rows], axis=0)
                     for t in range(nb)], axis=1)
                acc[p] = acc[p] * alpha + _dot(vt, pr)
                m[p] = mn
        return tuple(m), tuple(acc)

    n_grp = (i + 1) // KV_GROUP
    carry = (tuple(jnp.full((1, 2 * BLK), NEG, F32) for _ in pairs),
             tuple(jnp.zeros((LANES + SUM_ROWS, 2 * BLK), F32) for _ in pairs))
    carry = lax.fori_loop(0, n_grp, lambda jj, c: step(jj * KV_GROUP, c, KV_GROUP), carry)
    _, acc = lax.fori_loop(n_grp * KV_GROUP, i + 1, step, carry)
    for p in pairs:
        o_t = acc[p][:LANES] / acc[p][LANES:LANES + 1]
        o = jnp.concatenate([o_t[:HEAD, :BLK], o_t[HEAD:LANES, BLK:]], axis=0).T
        g = g_ref[0, :, ps[p]].astype(F32)
        o_ref[0, :, ps[p]] = (o * (g * _sigmoid(g))).astype(BF16)


def _fox_attn(p3, kext, v_t, qrows):
    B, L, _ = p3.shape
    n_pair = (p3.shape[2] // 3) // LANES
    nq = L // BLK
    n_p = ATTN_PAIRS
    width = n_p * LANES
    n_grp = n_pair // n_p
    assert n_grp == 1 and qrows.shape[2] == 2 * n_pair
    return pl.pallas_call(
        _fox_attn_kernel,
        grid=(B, n_grp, nq),
        in_specs=[
            pl.BlockSpec((1, BLK, width), lambda b, p, i: (b, i, p)),
            pl.BlockSpec((1, L, width), lambda b, p, i: (b, 0, n_grp + p)),
            pl.BlockSpec((1, L, width), lambda b, p, i: (b, 0, p)),
            pl.BlockSpec((1, nq, width, BLK), lambda b, p, i: (b, 0, p, 0)),
            pl.BlockSpec((1, BLK, width), lambda b, p, i: (b, i, 2 * n_grp + p)),
            pl.BlockSpec((1, 1, 2 * n_pair, LANES), lambda b, p, i: (b, i, 0, 0)),
        ],
        out_specs=pl.BlockSpec((1, BLK, width), lambda b, p, i: (b, i, p)),
        out_shape=jax.ShapeDtypeStruct((B, L, n_pair * LANES), BF16),
        compiler_params=pltpu.CompilerParams(
            dimension_semantics=("arbitrary", "arbitrary", "arbitrary"), vmem_limit_bytes=VMEM_LIMIT),
        name="fox_attn",
    )(p3, p3, kext, v_t, p3, qrows)


def _head_sums(x):
    row = lax.broadcasted_iota(jnp.int32, (LANES, LANES), 0)
    col = lax.broadcasted_iota(jnp.int32, (LANES, LANES), 1)
    ones = ((row // HEAD) == (col // HEAD)).astype(BF16)
    n_g, rows = x.shape[1] // LANES, x.shape[0]
    xs = jnp.concatenate([x[:, g * LANES:(g + 1) * LANES] for g in range(n_g)], axis=0).astype(BF16)
    s = _dot(xs, ones)
    return jnp.concatenate([s[g * rows:(g + 1) * rows] for g in range(n_g)], axis=1)


def _wkv_kernel(r_ref, k_ref, v_ref, g_ref, rh_ref, kh_ref, vh_ref, gh_ref, t_ref, th_ref,
                mu_ref, mut_ref, w0_ref, wup_ref, a0_ref, aup_ref, kk_ref, ka_ref, rk_ref,
                lnw_ref, lnb_ref, o_ref, st_ref, *, n_real):
    c = pl.program_id(1)

    @pl.when(c == 0)
    def _():
        st_ref[...] = jnp.zeros_like(st_ref)

    @pl.when(c >= n_real)
    def _():
        o_ref[...] = jnp.zeros_like(o_ref)

    @pl.when(c < n_real)
    def _():
        _wkv_chunk(r_ref, k_ref, v_ref, g_ref, rh_ref, kh_ref, vh_ref, gh_ref, t_ref, th_ref,
                   mu_ref, mut_ref, w0_ref, wup_ref, a0_ref, aup_ref, kk_ref, ka_ref, rk_ref,
                   lnw_ref, lnb_ref, o_ref, st_ref, c)


def _wkv_chunk(r_ref, k_ref, v_ref, g_ref, rh_ref, kh_ref, vh_ref, gh_ref, t_ref, th_ref,
               mu_ref, mut_ref, w0_ref, wup_ref, a0_ref, aup_ref, kk_ref, ka_ref, rk_ref,
               lnw_ref, lnb_ref, o_ref, st_ref, c):
    n_b = r_ref.shape[0]
    D = r_ref.shape[2]
    n_pair = D // LANES
    not_first = c != 0
    row = lax.broadcasted_iota(jnp.int32, (CHUNK, CHUNK), 0)
    col = lax.broadcasted_iota(jnp.int32, (CHUNK, CHUNK), 1)
    sub_diag = (col + 1 == row).astype(BF16)
    tril = (col <= row).astype(BF16)
    top0 = (lax.broadcasted_iota(jnp.int32, (8, 1), 0) == 0) & not_first
    trow0 = lax.broadcasted_iota(jnp.int32, (CHUNK, 1), 0) == 0

    def lerp(cur, sh, prev, mu):
        sh = jnp.concatenate([sh[0:8] + jnp.where(top0, prev, 0.0), sh[8:]], axis=0)
        return cur + (sh - cur) * mu

    def shifted(cur, prev, mu):
        return lerp(cur.astype(F32), _dot(sub_diag, cur), prev.astype(F32), mu)

    tls = []
    for b in range(n_b):
        tcur = t_ref[b]
        tls.append(lerp(tcur, jnp.where(trow0, 0.0, pltpu.roll(tcur, 1, axis=0)), th_ref[b, 7:8, :],
                        mut_ref[...]))
    tl = jnp.concatenate(tls, axis=0)
    w_log_all = w0_ref[...] + _dot(jnp.tanh(tl[:, :HEAD]).astype(BF16), wup_ref[...])
    a_all = _sigmoid(a0_ref[...] + _dot(tl[:, HEAD:].astype(BF16), aup_ref[...]))

    def prep(b):
        r = shifted(r_ref[b], rh_ref[b, 7:8, :], mu_ref[:, 0:D])
        k = shifted(k_ref[b], kh_ref[b, 7:8, :], mu_ref[:, D:2 * D])
        v = shifted(v_ref[b], vh_ref[b, 7:8, :], mu_ref[:, 2 * D:3 * D])
        g = shifted(g_ref[b], gh_ref[b, 7:8, :], mu_ref[:, 3 * D:4 * D])
        w_log = w_log_all[b * CHUNK:(b + 1) * CHUNK]
        a = a_all[b * CHUNK:(b + 1) * CHUNK]
        lw = (-0.5 * DECAY_SCALE * LOG2E) * jnp.tanh(0.5 * w_log) - 0.5 * DECAY_SCALE * LOG2E
        kk = k * kk_ref[...]
        kk = kk * lax.rsqrt(jnp.maximum(_head_sums(kk * kk), 1e-24))
        k = k * (1.0 + (a - 1.0) * ka_ref[...])
        hi, mid, lo = _split3(lw)
        cl = _dot(tril, hi) + _dot(tril, mid) + _dot(tril, lo)
        e_neg = jnp.exp2(-cl)
        return dict(
            r=r, k=k, v=v, g=g,
            r_t=(r * jnp.exp2(cl)).astype(BF16),
            a_t=(-kk * jnp.exp2(cl - lw)).astype(BF16),
            b_t=(kk * a * e_neg).astype(BF16),
            k_t=(k * e_neg).astype(BF16),
            w_end=jnp.exp2(cl[CHUNK - 1:CHUNK, :]),
            v_b=v.astype(BF16))

    pre = [prep(b) for b in range(n_b)]

    lane = lax.broadcasted_iota(jnp.int32, (1, LANES), 1)
    is_a = lane < HEAD
    trw = lax.broadcasted_iota(jnp.int32, (CHUNK, LANES), 0)
    jw = lax.broadcasted_iota(jnp.int32, (CHUNK, LANES), 1) % HEAD
    strict = jw < trw
    incl = jw <= trw
    r2 = lax.broadcasted_iota(jnp.int32, (LANES, LANES), 0) // HEAD
    c2 = lax.broadcasted_iota(jnp.int32, (LANES, LANES), 1) // HEAD
    same_head = r2 == c2
    zb16 = jnp.zeros((), BF16)

    def bdiag(xw, swap=False):
        za = jnp.where(is_a, xw, jnp.zeros_like(xw))
        zb = jnp.where(is_a, jnp.zeros_like(xw), xw)
        return jnp.concatenate([zb, za] if swap else [za, zb], axis=0)

    streams = [(b, p) for b in range(n_b) for p in range(n_pair)]
    ns = range(len(streams))

    def part(name, n):
        b, p = streams[n]
        return pre[b][name][:, p * LANES:(p + 1) * LANES]

    xq = [jnp.concatenate([part("a_t", n), part("r_t", n)], axis=0) for n in ns]
    ycat = [jnp.concatenate(
        [jnp.where(is_a, jnp.concatenate([part("k_t", n), part("b_t", n)], axis=0), zb16),
         jnp.where(is_a, zb16, jnp.concatenate([part("b_t", n), part("k_t", n)], axis=0))],
        axis=0) for n in ns]
    sc = [_dot_nt(xq[n], ycat[n]) for n in ns]
    s0 = [st_ref[b, p] for b, p in streams]
    xs = [_dot_nt(xq[n], s0[n].astype(BF16)) for n in ns]
    top_lo = [sc[n][:CHUNK, :LANES] for n in ns]
    top_hi = [sc[n][:CHUNK, LANES:] for n in ns]
    bot_lo = [sc[n][CHUNK:, :LANES] for n in ns]
    bot_hi = [sc[n][CHUNK:, LANES:] for n in ns]
    a_ak = [jnp.where(strict, jnp.where(is_a, top_lo[n], top_hi[n]), 0.0).astype(BF16) for n in ns]
    a_ab = [jnp.where(strict, jnp.where(is_a, top_hi[n], top_lo[n]), 0.0) for n in ns]
    m_rk = [jnp.where(incl, jnp.where(is_a, bot_lo[n], bot_hi[n]), 0.0).astype(BF16) for n in ns]
    m_rb = [jnp.where(incl, jnp.where(is_a, bot_hi[n], bot_lo[n]), 0.0).astype(BF16) for n in ns]

    eye = (jw == trw).astype(F32)
    lvl1 = (trw // 2 == jw // 2) & strict
    tw = [eye + jnp.where(lvl1, a_ab[n], 0.0) for n in ns]
    s = 2
    while s < CHUNK:
        off = (trw // (2 * s) == jw // (2 * s)) & ((trw // s) % 2 == 1) & ((jw // s) % 2 == 0)
        tb = [t.astype(BF16) for t in tw]
        pm = [_dot(tb[n], bdiag(jnp.where(off, a_ab[n], 0.0).astype(BF16))) for n in ns]
        tw = [tw[n] + _dot(pm[n].astype(BF16), bdiag(tb[n])) for n in ns]
        s *= 2

    vbd = [bdiag(part("v_b", n)) for n in ns]
    gm = [xs[n][:CHUNK] + _dot(a_ak[n], vbd[n]) for n in ns]
    u = [_dot(tw[n].astype(BF16), bdiag(gm[n].astype(BF16), swap=True)).astype(BF16) for n in ns]
    ys = [xs[n][CHUNK:] + _dot(jnp.concatenate([m_rb[n], m_rk[n]], axis=1),
                               jnp.concatenate([bdiag(u[n], swap=True), vbd[n]], axis=0))
          for n in ns]
    for n in ns:
        b, p = streams[n]
        uv = jnp.concatenate([u[n], part("v_b", n)], axis=0)
        bk = jnp.concatenate([part("b_t", n), part("k_t", n)], axis=0)
        st_ref[b, p] = (s0[n] + jnp.where(same_head, _dot_tn(uv, bk), 0.0)) * part("w_end", n)

    inv_n = 1.0 / HEAD
    for b in range(n_b):
        y = jnp.concatenate(ys[b * n_pair:(b + 1) * n_pair], axis=1)
        q = pre[b]
        mean = _head_sums(y) * inv_n
        d = y - mean
        var = _head_sums(d * d) * inv_n
        yn = d * lax.rsqrt(var + GN_EPS) * lnw_ref[...] + lnb_ref[...]
        bonus = _head_sums(q["r"] * q["k"] * rk_ref[...]) * q["v"]
        o_ref[b] = ((yn + bonus) * (q["g"] * _sigmoid(q["g"]))).astype(BF16)


def _wkv(p3, tail3, n_real, mu, mu_t, w0, w_up, a0, a_up, k_k, k_a, r_k, ln_w, ln_b):
    B, L, n_main = p3.shape
    D = n_main // 4
    nc = L // CHUNK
    sub = CHUNK // 8
    n_b = WKV_BATCH if B % WKV_BATCH == 0 else 1

    def cur(s):
        return pl.BlockSpec((n_b, CHUNK, D), lambda b, c: (b, c, s))

    def halo(s):
        return pl.BlockSpec((n_b, 8, D), lambda b, c: (b, jnp.maximum(c * sub - 1, 0), s))

    def vec(n):
        return pl.BlockSpec((1, n), lambda b, c: (0, 0))

    lora = pl.BlockSpec((HEAD, D), lambda b, c: (0, 0))
    return pl.pallas_call(
        functools.partial(_wkv_kernel, n_real=n_real),
        grid=(B // n_b, nc),
        in_specs=[cur(0), cur(1), cur(2), cur(3), halo(0), halo(1), halo(2), halo(3),
                  pl.BlockSpec((n_b, CHUNK, LANES), lambda b, c: (b, c, 0)),
                  pl.BlockSpec((n_b, 8, LANES), lambda b, c: (b, jnp.maximum(c * sub - 1, 0), 0)),
                  vec(n_main), vec(LANES), vec(D), lora, vec(D), lora,
                  vec(D), vec(D), vec(D), vec(D), vec(D)],
        out_specs=pl.BlockSpec((n_b, CHUNK, D), lambda b, c: (b, c, 0)),
        out_shape=jax.ShapeDtypeStruct((B, L, D), BF16),
        scratch_shapes=[pltpu.VMEM((n_b, D // LANES, LANES, LANES), F32)],
        compiler_params=pltpu.CompilerParams(
            dimension_semantics=("arbitrary", "arbitrary"), vmem_limit_bytes=VMEM_LIMIT),
        name="wkv7",
    )(p3, p3, p3, p3, p3, p3, p3, p3, tail3, tail3,
      mu, mu_t, w0, w_up, a0, a_up, k_k, k_a, r_k, ln_w, ln_b)


def _fox_mixer(p, tail, v_t, B, L, b_f):
    D = p.shape[1] // 3
    n_heads = b_f.shape[0]
    b_row = jnp.pad(b_f, (0, LANES - n_heads)).reshape(1, LANES)
    kext, qrows = _fox_bias(tail.reshape(B, L, LANES), b_row, n_heads, D)
    return _fox_attn(p.reshape(B, L, 3 * D), kext, v_t.reshape(B, L // BLK, D, BLK),
                     qrows).reshape(B * L, D)


def _rwkv_mixer(p, tail, B, L, L0, mu, w0, w_up, a0, a_up, k_k, k_a, r_k, ln_w, ln_b):
    n_main = p.shape[1]
    row = lambda t: t.reshape(1, -1)
    z = _wkv(p.reshape(B, L, n_main), tail.reshape(B, L, LANES), -(-L0 // CHUNK),
             row(mu[:n_main]), row(mu[n_main:]), row(w0), w_up.astype(BF16), row(a0),
             a_up.astype(BF16), row(k_k), row(k_a), row(r_k), row(ln_w), row(ln_b))
    return z.reshape(B * L, n_main // 4)


def kernel(x, meta_tokens, norm_pre, norm_post, fox_w_in, fox_b_f, fox_w_out, rwkv_w_in, rwkv_mu, rwkv_w0, rwkv_w_up, rwkv_a0, rwkv_a_up, rwkv_k_k, rwkv_k_a, rwkv_r_k, rwkv_ln_w, rwkv_ln_b, rwkv_w_out):
    B, S, D = x.shape
    depth = norm_pre.shape[0]
    n_heads = fox_b_f.shape[1]
    L0 = N_META + S
    L = -(-L0 // BLK) * BLK
    assert (B * L) % ROW_TILE == 0 and D % LANES == 0
    meta = jnp.broadcast_to(meta_tokens[None].astype(x.dtype), (B, N_META, D))
    h = jnp.concatenate([meta, x, jnp.zeros((B, L - L0, D), x.dtype)], axis=1).reshape(B * L, D)

    def in_proj_args(i):
        g = norm_pre[i].reshape(1, D)
        if i % 2 == 0:
            wf = fox_w_in[i // 2]
            w = jnp.concatenate([wf[:, :2 * D], wf[:, 3 * D:], jnp.zeros((D, LANES - n_heads), F32)],
                                axis=1)
            return ((w.astype(BF16), wf[:, 2 * D:3 * D].T.astype(BF16)), g,
                    dict(q_cols=D, q_scale=LOG2E * HEAD ** -0.5))
        return (rwkv_w_in[i // 2].astype(BF16),), g, {}

    ws, g, qkw = in_proj_args(0)
    p, tail, *v_t = _norm_proj(h, g, *ws, **qkw)
    for i in range(depth):
        j = i // 2
        if i % 2 == 0:
            z = _fox_mixer(p, tail, v_t[0], B, L, fox_b_f[j])
            w_out = fox_w_out[j]
        else:
            z = _rwkv_mixer(p, tail, B, L, L0, rwkv_mu[j], rwkv_w0[j], rwkv_w_up[j], rwkv_a0[j],
                            rwkv_a_up[j], rwkv_k_k[j], rwkv_k_a[j], rwkv_r_k[j], rwkv_ln_w[j],
                            rwkv_ln_b[j])
            w_out = rwkv_w_out[j]
        g_post = norm_post[i].reshape(1, D)
        if i + 1 < depth:
            ws, g, qkw = in_proj_args(i + 1)
            h, p, tail, *v_t = _out_norm_proj(z, w_out.astype(BF16), h, g_post, g, *ws, **qkw)
        elif S % ROW_TILE == 0:
            return _out_proj_final(z, w_out.astype(BF16), h, g_post, B, L, S).reshape(B, S, D)
        else:
            h = _out_proj(z, w_out.astype(BF16), h, g_post)
    return h.reshape(B, L, D)[:, N_META:L0]
```

```python
import functools
import math

import jax
import jax.numpy as jnp
from jax import lax
from jax.experimental import pallas as pl
from jax.experimental.pallas import tpu as pltpu

F32 = jnp.float32
BF16 = jnp.bfloat16

N_META = 16
HEAD = 64
LANES = 128
BLK = 128
ATTN_PAIRS = 8
ATTN_SUBGROUP = 8
KV_GROUP = 4
SUM_ROWS = 16
CHUNK = 64
WKV_BATCH = 8
ROW_TILE = 512
COL_TILE = 512
NORM_EPS = 1e-6
GN_EPS = 64e-5
DECAY_SCALE = math.exp(-0.5)
LOG2E = math.log2(math.e)
NEG = -1e30
VMEM_LIMIT = 48 * 1024 * 1024
VMEM_LIMIT_FUSED = 56 * 1024 * 1024


def _dot(a, b):
    return jnp.dot(a, b, preferred_element_type=F32)


def _dot_nt(a, b):
    return lax.dot_general(a, b, (((1,), (1,)), ((), ())), preferred_element_type=F32)


def _dot_tn(a, b):
    return lax.dot_general(a, b, (((0,), (0,)), ((), ())), preferred_element_type=F32)


def _split3(x):
    hi = x.astype(BF16)
    r1 = x - hi.astype(F32)
    mid = r1.astype(BF16)
    lo = (r1 - mid.astype(F32)).astype(BF16)
    return hi, mid, lo


def _sigmoid(x):
    return 0.5 * jnp.tanh(0.5 * x) + 0.5


def _project(x, g_ref, w_refs, outs, n_main, q_cols, q_scale):
    var = jnp.mean(x * x, axis=-1, keepdims=True)
    u = (x * lax.rsqrt(var + NORM_EPS) * g_ref[...]).astype(BF16)
    w_ref, (o_ref, t_ref) = w_refs[0], outs[:2]
    for j in range(0, n_main, COL_TILE):
        acc = _dot(u, w_ref[:, j:j + COL_TILE])
        if j < q_cols:
            acc = acc * q_scale
        o_ref[:, j:j + COL_TILE] = acc.astype(BF16)
    t_ref[...] = _dot(u, w_ref[:, n_main:])
    if len(w_refs) > 1:
        for j in range(0, w_refs[1].shape[0], COL_TILE):
            acc_t = _dot_nt(w_refs[1][j:j + COL_TILE, :], u).astype(BF16)
            for t in range(x.shape[0] // BLK):
                outs[2][t, j:j + COL_TILE, :] = acc_t[:, t * BLK:(t + 1) * BLK]


def _proj_specs(T, w, w_t):
    n_main = w.shape[1] - LANES
    whole = lambda a: pl.BlockSpec(a.shape, lambda i: (0,) * a.ndim)
    in_specs = [whole(w)]
    out_specs = [pl.BlockSpec((ROW_TILE, n_main), lambda i: (i, 0)),
                 pl.BlockSpec((ROW_TILE, LANES), lambda i: (i, 0))]
    shapes = [jax.ShapeDtypeStruct((T, n_main), BF16), jax.ShapeDtypeStruct((T, LANES), F32)]
    if w_t is not None:
        in_specs.append(whole(w_t))
        out_specs.append(pl.BlockSpec((ROW_TILE // BLK, w_t.shape[0], BLK), lambda i: (i, 0, 0)))
        shapes.append(jax.ShapeDtypeStruct((T // BLK, w_t.shape[0], BLK), BF16))
    return in_specs, out_specs, shapes


def _norm_proj_kernel(h_ref, g_ref, *refs, n_w, n_main, q_cols, q_scale):
    _project(h_ref[...], g_ref, refs[:n_w], refs[n_w:], n_main, q_cols, q_scale)


def _norm_proj(h, g, w, w_t=None, *, q_cols=0, q_scale=1.0):
    T, D = h.shape
    ws = [w] if w_t is None else [w, w_t]
    w_specs, out_specs, out_shape = _proj_specs(T, w, w_t)
    kern = functools.partial(_norm_proj_kernel, n_w=len(ws), n_main=w.shape[1] - LANES,
                             q_cols=q_cols, q_scale=q_scale)
    return pl.pallas_call(
        kern,
        grid=(T // ROW_TILE,),
        in_specs=[pl.BlockSpec((ROW_TILE, D), lambda i: (i, 0)),
                  pl.BlockSpec((1, D), lambda i: (0, 0))] + w_specs,
        out_specs=out_specs,
        out_shape=out_shape,
        compiler_params=pltpu.CompilerParams(
            dimension_semantics=("arbitrary",), vmem_limit_bytes=VMEM_LIMIT),
        name="norm_proj",
    )(h, g, *ws)


def _residual(z_ref, w_ref, h_ref, g_ref):
    m = _dot(z_ref[...], w_ref[...])
    var = jnp.mean(m * m, axis=-1, keepdims=True)
    return h_ref[...] + m * lax.rsqrt(var + NORM_EPS) * g_ref[...]


def _out_proj_kernel(z_ref, w_ref, h_ref, g_ref, o_ref):
    o_ref[...] = _residual(z_ref, w_ref, h_ref, g_ref)


def _out_norm_proj_kernel(z_ref, wo_ref, h_ref, gpost_ref, gpre_ref, *refs,
                          n_w, n_main, q_cols, q_scale):
    hn = _residual(z_ref, wo_ref, h_ref, gpost_ref)
    refs[n_w][...] = hn
    _project(hn, gpre_ref, refs[:n_w], refs[n_w + 1:], n_main, q_cols, q_scale)


def _out_norm_proj(z, w_out, h, g_post, g_pre, w, w_t=None, *, q_cols=0, q_scale=1.0):
    T, D = h.shape
    ws = [w] if w_t is None else [w, w_t]
    w_specs, out_specs, out_shape = _proj_specs(T, w, w_t)
    kern = functools.partial(_out_norm_proj_kernel, n_w=len(ws), n_main=w.shape[1] - LANES,
                             q_cols=q_cols, q_scale=q_scale)
    rows = lambda n: pl.BlockSpec((ROW_TILE, n), lambda i: (i, 0))
    whole = lambda a: pl.BlockSpec(a.shape, lambda i: (0, 0))
    return pl.pallas_call(
        kern,
        grid=(T // ROW_TILE,),
        in_specs=[rows(D), whole(w_out), rows(D), whole(g_post), whole(g_pre)] + w_specs,
        out_specs=[rows(D)] + out_specs,
        out_shape=[jax.ShapeDtypeStruct((T, D), F32)] + out_shape,
        compiler_params=pltpu.CompilerParams(
            dimension_semantics=("arbitrary",), vmem_limit_bytes=VMEM_LIMIT_FUSED),
        name="out_norm_proj",
    )(z, w_out, h, g_post, g_pre, *ws)


def _out_proj(z, w, h, g):
    T, D = h.shape
    return pl.pallas_call(
        _out_proj_kernel,
        grid=(T // ROW_TILE,),
        in_specs=[
            pl.BlockSpec((ROW_TILE, D), lambda i: (i, 0)),
            pl.BlockSpec((D, D), lambda i: (0, 0)),
            pl.BlockSpec((ROW_TILE, D), lambda i: (i, 0)),
            pl.BlockSpec((1, D), lambda i: (0, 0)),
        ],
        out_specs=pl.BlockSpec((ROW_TILE, D), lambda i: (i, 0)),
        out_shape=jax.ShapeDtypeStruct((T, D), F32),
        compiler_params=pltpu.CompilerParams(
            dimension_semantics=("arbitrary",), vmem_limit_bytes=VMEM_LIMIT),
        name="out_proj",
    )(z, w, h, g)


def _out_proj_final(z, w, h, g, B, L, S):
    D = h.shape[1]
    n_r = S // ROW_TILE
    rows_in = pl.BlockSpec((pl.Element(ROW_TILE), pl.Element(D)),
                           lambda b, r: (pl.multiple_of(b * L + N_META + r * ROW_TILE, 8), 0))
    return pl.pallas_call(
        _out_proj_kernel,
        grid=(B, n_r),
        in_specs=[rows_in, pl.BlockSpec((D, D), lambda b, r: (0, 0)), rows_in,
                  pl.BlockSpec((1, D), lambda b, r: (0, 0))],
        out_specs=pl.BlockSpec((ROW_TILE, D), lambda b, r: (b * n_r + r, 0)),
        out_shape=jax.ShapeDtypeStruct((B * S, D), F32),
        compiler_params=pltpu.CompilerParams(
            dimension_semantics=("arbitrary", "arbitrary"), vmem_limit_bytes=VMEM_LIMIT),
        name="out_proj_final",
    )(z, w, h, g)


def _fox_bias_kernel(t_ref, b_ref, x_ref, q_ref):
    L = t_ref.shape[1]
    width = x_ref.shape[2]
    n_heads = q_ref.shape[2]
    x = t_ref[0] + b_ref[...]
    lf = jnp.minimum(x, 0.0) - jnp.log(1.0 + jnp.exp(-jnp.abs(x)))
    row = lax.broadcasted_iota(jnp.int32, (BLK, BLK), 0)
    col = lax.broadcasted_iota(jnp.int32, (BLK, BLK), 1)
    tril = (col <= row).astype(BF16)
    carry = jnp.zeros((1, LANES), F32)
    blocks = []
    for j in range(L // BLK):
        hi, mid, lo = _split3(lf[j * BLK:(j + 1) * BLK])
        blk = _dot(tril, hi) + _dot(tril, mid) + _dot(tril, lo) + carry
        carry = blk[BLK - 1:BLK]
        blocks.append(blk)
    cum2 = jnp.concatenate(blocks, axis=0) * LOG2E

    head = lax.broadcasted_iota(jnp.int32, (LANES, width), 0)
    colw = lax.broadcasted_iota(jnp.int32, (LANES, width), 1)
    base = (head // 2) * LANES + (head % 2) * 3
    ext = jnp.zeros((L, width), F32)
    for o, term in enumerate(_split3(-cum2)):
        ext = ext + _dot(term, ((head < n_heads) & (colw == base + o)).astype(BF16))
    lane_w = lax.broadcasted_iota(jnp.int32, (1, width), 1) % LANES
    x_ref[0] = jnp.where((lane_w >= 6) & (lane_w < 9), 1.0, ext).astype(BF16)

    hrow = lax.broadcasted_iota(jnp.int32, (n_heads, LANES), 0)
    lane = lax.broadcasted_iota(jnp.int32, (n_heads, LANES), 1)
    slot = (hrow % 2) * 3
    own = ((lane >= slot) & (lane < slot + 3)).astype(F32)
    for j in range(L // BLK):
        diag = jnp.where(hrow == lane, blocks[j][0:1] * LOG2E, 0.0)
        rows = own
        for o, term in enumerate(_split3(diag)):
            rows = rows + _dot(term, (col == 6 + o).astype(BF16))
        q_ref[0, j] = rows


def _fox_bias(tail3, b_row, n_heads, width):
    B, L, _ = tail3.shape
    nq = L // BLK
    return pl.pallas_call(
        _fox_bias_kernel,
        grid=(B,),
        in_specs=[pl.BlockSpec((1, L, LANES), lambda b: (b, 0, 0)),
                  pl.BlockSpec((1, LANES), lambda b: (0, 0))],
        out_specs=[pl.BlockSpec((1, L, width), lambda b: (b, 0, 0)),
                   pl.BlockSpec((1, nq, n_heads, LANES), lambda b: (b, 0, 0, 0))],
        out_shape=[jax.ShapeDtypeStruct((B, L, width), BF16),
                   jax.ShapeDtypeStruct((B, nq, n_heads, LANES), F32)],
        compiler_params=pltpu.CompilerParams(
            dimension_semantics=("arbitrary",), vmem_limit_bytes=VMEM_LIMIT),
        name="fox_bias",
    )(tail3, b_row)


def _fox_attn_kernel(q_ref, k_ref, x_ref, vt_ref, g_ref, qx_ref, o_ref):
    i = pl.program_id(2)
    n_p = q_ref.shape[2] // LANES
    pairs = range(n_p)
    ps = [slice(p * LANES, (p + 1) * LANES) for p in pairs]
    lane = lax.broadcasted_iota(jnp.int32, (1, LANES), 1)
    is_a = lane < HEAD
    zq = jnp.zeros((), BF16)
    ones_rows = jnp.ones((SUM_ROWS, BLK), BF16)

    def ext_rows(h):
        return jnp.broadcast_to(qx_ref[0, 0, h:h + 1, :], (BLK, LANES)).astype(BF16)

    qx = []
    for p in pairs:
        q2 = q_ref[0, :, ps[p]]
        qx.append(jnp.concatenate(
            [jnp.concatenate([jnp.where(is_a, q2, zq), ext_rows(2 * p)], axis=1),
             jnp.concatenate([jnp.where(is_a, zq, q2), ext_rows(2 * p + 1)], axis=1)], axis=0))

    key_i = lax.broadcasted_iota(jnp.int32, (BLK, 2 * BLK), 0)
    qry_i = lax.broadcasted_iota(jnp.int32, (BLK, 2 * BLK), 1) % BLK
    keep = key_i <= qry_i

    def step(j, carry, nb=1):
        m, acc = carry
        off = pl.multiple_of(j * BLK, BLK)
        unmasked = jnp.logical_or(keep, j + (nb - 1) != i)
        m, acc = list(m), list(acc)
        n_sub = ATTN_SUBGROUP if nb > 1 else n_p
        for g0 in range(0, n_p, n_sub):
            sub = range(g0, min(g0 + n_sub, n_p))
            st = {}
            for p in sub:
                kx = jnp.concatenate([k_ref[0, pl.ds(off, nb * BLK), ps[p]],
                                      x_ref[0, pl.ds(off, nb * BLK), ps[p]]], axis=1)
                s = _dot_nt(kx, qx[p])
                last = jnp.where(unmasked, s[(nb - 1) * BLK:], NEG)
                st[p] = last if nb == 1 else jnp.concatenate([s[:(nb - 1) * BLK], last], axis=0)
            for p in sub:
                mn = jnp.maximum(m[p], jnp.max(st[p], axis=0, keepdims=True))
                alpha = jnp.exp2(m[p] - mn)
                pr = jnp.exp2(st[p] - mn).astype(BF16)
                vt = jnp.concatenate(
                    [jnp.concatenate([vt_ref[0, j + t, ps[p], :], ones_rows], axis=0)
                     for t in range(nb)], axis=1)
                acc[p] = acc[p] * alpha + _dot(vt, pr)
                m[p] = mn
        return tuple(m), tuple(acc)

    n_grp = (i + 1) // KV_GROUP
    carry = (tuple(jnp.full((1, 2 * BLK), NEG, F32) for _ in pairs),
             tuple(jnp.zeros((LANES + SUM_ROWS, 2 * BLK), F32) for _ in pairs))
    carry = lax.fori_loop(0, n_grp, lambda jj, c: step(jj * KV_GROUP, c, KV_GROUP), carry)
    _, acc = lax.fori_loop(n_grp * KV_GROUP, i + 1, step, carry)
    for p in pairs:
        o_t = acc[p][:LANES] / acc[p][LANES:LANES + 1]
        o = jnp.concatenate([o_t[:HEAD, :BLK], o_t[HEAD:LANES, BLK:]], axis=0).T
        g = g_ref[0, :, ps[p]].astype(F32)
        o_ref[0, :, ps[p]] = (o * (g * _sigmoid(g))).astype(BF16)


def _fox_attn(p3, kext, v_t, qrows):
    B, L, _ = p3.shape
    n_pair = (p3.shape[2] // 3) // LANES
    nq = L // BLK
    n_p = ATTN_PAIRS
    width = n_p * LANES
    n_grp = n_pair // n_p
    assert n_grp == 1 and qrows.shape[2] == 2 * n_pair
    return pl.pallas_call(
        _fox_attn_kernel,
        grid=(B, n_grp, nq),
        in_specs=[
            pl.BlockSpec((1, BLK, width), lambda b, p, i: (b, i, p)),
            pl.BlockSpec((1, L, width), lambda b, p, i: (b, 0, n_grp + p)),
            pl.BlockSpec((1, L, width), lambda b, p, i: (b, 0, p)),
            pl.BlockSpec((1, nq, width, BLK), lambda b, p, i: (b, 0, p, 0)),
            pl.BlockSpec((1, BLK, width), lambda b, p, i: (b, i, 2 * n_grp + p)),
            pl.BlockSpec((1, 1, 2 * n_pair, LANES), lambda b, p, i: (b, i, 0, 0)),
        ],
        out_specs=pl.BlockSpec((1, BLK, width), lambda b, p, i: (b, i, p)),
        out_shape=jax.ShapeDtypeStruct((B, L, n_pair * LANES), BF16),
        compiler_params=pltpu.CompilerParams(
            dimension_semantics=("arbitrary", "arbitrary", "arbitrary"), vmem_limit_bytes=VMEM_LIMIT),
        name="fox_attn",
    )(p3, p3, kext, v_t, p3, qrows)


def _head_sums(x):
    row = lax.broadcasted_iota(jnp.int32, (LANES, LANES), 0)
    col = lax.broadcasted_iota(jnp.int32, (LANES, LANES), 1)
    ones = ((row // HEAD) == (col // HEAD)).astype(BF16)
    n_g, rows = x.shape[1] // LANES, x.shape[0]
    xs = jnp.concatenate([x[:, g * LANES:(g + 1) * LANES] for g in range(n_g)], axis=0).astype(BF16)
    s = _dot(xs, ones)
    return jnp.concatenate([s[g * rows:(g + 1) * rows] for g in range(n_g)], axis=1)


def _wkv_kernel(r_ref, k_ref, v_ref, g_ref, rh_ref, kh_ref, vh_ref, gh_ref, t_ref, th_ref,
                mu_ref, mut_ref, w0_ref, wup_ref, a0_ref, aup_ref, kk_ref, ka_ref, rk_ref,
                lnw_ref, lnb_ref, o_ref, st_ref, *, n_real):
    c = pl.program_id(1)

    @pl.when(c == 0)
    def _():
        st_ref[...] = jnp.zeros_like(st_ref)

    @pl.when(c >= n_real)
    def _():
        o_ref[...] = jnp.zeros_like(o_ref)

    @pl.when(c < n_real)
    def _():
        _wkv_chunk(r_ref, k_ref, v_ref, g_ref, rh_ref, kh_ref, vh_ref, gh_ref, t_ref, th_ref,
                   mu_ref, mut_ref, w0_ref, wup_ref, a0_ref, aup_ref, kk_ref, ka_ref, rk_ref,
                   lnw_ref, lnb_ref, o_ref, st_ref, c)


def _wkv_chunk(r_ref, k_ref, v_ref, g_ref, rh_ref, kh_ref, vh_ref, gh_ref, t_ref, th_ref,
               mu_ref, mut_ref, w0_ref, wup_ref, a0_ref, aup_ref, kk_ref, ka_ref, rk_ref,
               lnw_ref, lnb_ref, o_ref, st_ref, c):
    n_b = r_ref.shape[0]
    D = r_ref.shape[2]
    n_pair = D // LANES
    not_first = c != 0
    row = lax.broadcasted_iota(jnp.int32, (CHUNK, CHUNK), 0)
    col = lax.broadcasted_iota(jnp.int32, (CHUNK, CHUNK), 1)
    sub_diag = (col + 1 == row).astype(BF16)
    tril = (col <= row).astype(BF16)
    top0 = (lax.broadcasted_iota(jnp.int32, (8, 1), 0) == 0) & not_first
    trow0 = lax.broadcasted_iota(jnp.int32, (CHUNK, 1), 0) == 0

    def lerp(cur, sh, prev, mu):
        sh = jnp.concatenate([sh[0:8] + jnp.where(top0, prev, 0.0), sh[8:]], axis=0)
        return cur + (sh - cur) * mu

    def shifted(cur, prev, mu):
        return lerp(cur.astype(F32), _dot(sub_diag, cur), prev.astype(F32), mu)

    tls = []
    for b in range(n_b):
        tcur = t_ref[b]
        tls.append(lerp(tcur, jnp.where(trow0, 0.0, pltpu.roll(tcur, 1, axis=0)), th_ref[b, 7:8, :],
                        mut_ref[...]))
    tl = jnp.concatenate(tls, axis=0)
    w_log_all = w0_ref[...] + _dot(jnp.tanh(tl[:, :HEAD]).astype(BF16), wup_ref[...])
    a_all = _sigmoid(a0_ref[...] + _dot(tl[:, HEAD:].astype(BF16), aup_ref[...]))

    def prep(b):
        r = shifted(r_ref[b], rh_ref[b, 7:8, :], mu_ref[:, 0:D])
        k = shifted(k_ref[b], kh_ref[b, 7:8, :], mu_ref[:, D:2 * D])
        v = shifted(v_ref[b], vh_ref[b, 7:8, :], mu_ref[:, 2 * D:3 * D])
        g = shifted(g_ref[b], gh_ref[b, 7:8, :], mu_ref[:, 3 * D:4 * D])
        w_log = w_log_all[b * CHUNK:(b + 1) * CHUNK]
        a = a_all[b * CHUNK:(b + 1) * CHUNK]
        lw = (-0.5 * DECAY_SCALE * LOG2E) * jnp.tanh(0.5 * w_log) - 0.5 * DECAY_SCALE * LOG2E
        kk = k * kk_ref[...]
        kk = kk * lax.rsqrt(jnp.maximum(_head_sums(kk * kk), 1e-24))
        k = k * (1.0 + (a - 1.0) * ka_ref[...])
        hi, mid, lo = _split3(lw)
        cl = _dot(tril, hi) + _dot(tril, mid) + _dot(tril, lo)
        e_neg = jnp.exp2(-cl)
        return dict(
            r=r, k=k, v=v, g=g,
            r_t=(r * jnp.exp2(cl)).astype(BF16),
            a_t=(-kk * jnp.exp2(cl - lw)).astype(BF16),
            b_t=(kk * a * e_neg).astype(BF16),
            k_t=(k * e_neg).astype(BF16),
            w_end=jnp.exp2(cl[CHUNK - 1:CHUNK, :]),
            v_b=v.astype(BF16))

    pre = [prep(b) for b in range(n_b)]

    lane = lax.broadcasted_iota(jnp.int32, (1, LANES), 1)
    is_a = lane < HEAD
    trw = lax.broadcasted_iota(jnp.int32, (CHUNK, LANES), 0)
    jw = lax.broadcasted_iota(jnp.int32, (CHUNK, LANES), 1) % HEAD
    strict = jw < trw
    incl = jw <= trw
    r2 = lax.broadcasted_iota(jnp.int32, (LANES, LANES), 0) // HEAD
    c2 = lax.broadcasted_iota(jnp.int32, (LANES, LANES), 1) // HEAD
    same_head = r2 == c2
    zb16 = jnp.zeros((), BF16)

    def bdiag(xw, swap=False):
        za = jnp.where(is_a, xw, jnp.zeros_like(xw))
        zb = jnp.where(is_a, jnp.zeros_like(xw), xw)
        return jnp.concatenate([zb, za] if swap else [za, zb], axis=0)

    streams = [(b, p) for b in range(n_b) for p in range(n_pair)]
    ns = range(len(streams))

    def part(name, n):
        b, p = streams[n]
        return pre[b][name][:, p * LANES:(p + 1) * LANES]

    xq = [jnp.concatenate([part("a_t", n), part("r_t", n)], axis=0) for n in ns]
    ycat = [jnp.concatenate(
        [jnp.where(is_a, jnp.concatenate([part("k_t", n), part("b_t", n)], axis=0), zb16),
         jnp.where(is_a, zb16, jnp.concatenate([part("b_t", n), part("k_t", n)], axis=0))],
        axis=0) for n in ns]
    sc = [_dot_nt(xq[n], ycat[n]) for n in ns]
    s0 = [st_ref[b, p] for b, p in streams]
    xs = [_dot_nt(xq[n], s0[n].astype(BF16)) for n in ns]
    top_lo = [sc[n][:CHUNK, :LANES] for n in ns]
    top_hi = [sc[n][:CHUNK, LANES:] for n in ns]
    bot_lo = [sc[n][CHUNK:, :LANES] for n in ns]
    bot_hi = [sc[n][CHUNK:, LANES:] for n in ns]
    a_ak = [jnp.where(strict, jnp.where(is_a, top_lo[n], top_hi[n]), 0.0).astype(BF16) for n in ns]
    a_ab = [jnp.where(strict, jnp.where(is_a, top_hi[n], top_lo[n]), 0.0) for n in ns]
    m_rk = [jnp.where(incl, jnp.where(is_a, bot_lo[n], bot_hi[n]), 0.0).astype(BF16) for n in ns]
    m_rb = [jnp.where(incl, jnp.where(is_a, bot_hi[n], bot_lo[n]), 0.0).astype(BF16) for n in ns]

    eye = (jw == trw).astype(F32)
    lvl1 = (trw // 2 == jw // 2) & strict
    tw = [eye + jnp.where(lvl1, a_ab[n], 0.0) for n in ns]
    s = 2
    while s < CHUNK:
        off = (trw // (2 * s) == jw // (2 * s)) & ((trw // s) % 2 == 1) & ((jw // s) % 2 == 0)
        tb = [t.astype(BF16) for t in tw]
        pm = [_dot(tb[n], bdiag(jnp.where(off, a_ab[n], 0.0).astype(BF16))) for n in ns]
        tw = [tw[n] + _dot(pm[n].astype(BF16), bdiag(tb[n])) for n in ns]
        s *= 2

    vbd = [bdiag(part("v_b", n)) for n in ns]
    gm = [xs[n][:CHUNK] + _dot(a_ak[n], vbd[n]) for n in ns]
    u = [_dot(tw[n].astype(BF16), bdiag(gm[n].astype(BF16), swap=True)).astype(BF16) for n in ns]
    ys = [xs[n][CHUNK:] + _dot(jnp.concatenate([m_rb[n], m_rk[n]], axis=1),
                               jnp.concatenate([bdiag(u[n], swap=True), vbd[n]], axis=0))
          for n in ns]
    for n in ns:
        b, p = streams[n]
        uv = jnp.concatenate([u[n], part("v_b", n)], axis=0)
        bk = jnp.concatenate([part("b_t", n), part("k_t", n)], axis=0)
        st_ref[b, p] = (s0[n] + jnp.where(same_head, _dot_tn(uv, bk), 0.0)) * part("w_end", n)

    inv_n = 1.0 / HEAD
    for b in range(n_b):
        y = jnp.concatenate(ys[b * n_pair:(b + 1) * n_pair], axis=1)
        q = pre[b]
        mean = _head_sums(y) * inv_n
        d = y - mean
        var = _head_sums(d * d) * inv_n
        yn = d * lax.rsqrt(var + GN_EPS) * lnw_ref[...] + lnb_ref[...]
        bonus = _head_sums(q["r"] * q["k"] * rk_ref[...]) * q["v"]
        o_ref[b] = ((yn + bonus) * (q["g"] * _sigmoid(q["g"]))).astype(BF16)


def _wkv(p3, tail3, n_real, mu, mu_t, w0, w_up, a0, a_up, k_k, k_a, r_k, ln_w, ln_b):
    B, L, n_main = p3.shape
    D = n_main // 4
    nc = L // CHUNK
    sub = CHUNK // 8
    n_b = WKV_BATCH if B % WKV_BATCH == 0 else 1

    def cur(s):
        return pl.BlockSpec((n_b, CHUNK, D), lambda b, c: (b, c, s))

    def halo(s):
        return pl.BlockSpec((n_b, 8, D), lambda b, c: (b, jnp.maximum(c * sub - 1, 0), s))

    def vec(n):
        return pl.BlockSpec((1, n), lambda b, c: (0, 0))

    lora = pl.BlockSpec((HEAD, D), lambda b, c: (0, 0))
    return pl.pallas_call(
        functools.partial(_wkv_kernel, n_real=n_real),
        grid=(B // n_b, nc),
        in_specs=[cur(0), cur(1), cur(2), cur(3), halo(0), halo(1), halo(2), halo(3),
                  pl.BlockSpec((n_b, CHUNK, LANES), lambda b, c: (b, c, 0)),
                  pl.BlockSpec((n_b, 8, LANES), lambda b, c: (b, jnp.maximum(c * sub - 1, 0), 0)),
                  vec(n_main), vec(LANES), vec(D), lora, vec(D), lora,
                  vec(D), vec(D), vec(D), vec(D), vec(D)],
        out_specs=pl.BlockSpec((n_b, CHUNK, D), lambda b, c: (b, c, 0)),
        out_shape=jax.ShapeDtypeStruct((B, L, D), BF16),
        scratch_shapes=[pltpu.VMEM((n_b, D // LANES, LANES, LANES), F32)],
        compiler_params=pltpu.CompilerParams(
            dimension_semantics=("arbitrary", "arbitrary"), vmem_limit_bytes=VMEM_LIMIT),
        name="wkv7",
    )(p3, p3, p3, p3, p3, p3, p3, p3, tail3, tail3,
      mu, mu_t, w0, w_up, a0, a_up, k_k, k_a, r_k, ln_w, ln_b)


def _fox_mixer(p, tail, v_t, B, L, b_f):
    D = p.shape[1] // 3
    n_heads = b_f.shape[0]
    b_row = jnp.pad(b_f, (0, LANES - n_heads)).reshape(1, LANES)
    kext, qrows = _fox_bias(tail.reshape(B, L, LANES), b_row, n_heads, D)
    return _fox_attn(p.reshape(B, L, 3 * D), kext, v_t.reshape(B, L // BLK, D, BLK),
                     qrows).reshape(B * L, D)


def _rwkv_mixer(p, tail, B, L, L0, mu, w0, w_up, a0, a_up, k_k, k_a, r_k, ln_w, ln_b):
    n_main = p.shape[1]
    row = lambda t: t.reshape(1, -1)
    z = _wkv(p.reshape(B, L, n_main), tail.reshape(B, L, LANES), -(-L0 // CHUNK),
             row(mu[:n_main]), row(mu[n_main:]), row(w0), w_up.astype(BF16), row(a0),
             a_up.astype(BF16), row(k_k), row(k_a), row(r_k), row(ln_w), row(ln_b))
    return z.reshape(B * L, n_main // 4)


def kernel(x, meta_tokens, norm_pre, norm_post, fox_w_in, fox_b_f, fox_w_out, rwkv_w_in, rwkv_mu, rwkv_w0, rwkv_w_up, rwkv_a0, rwkv_a_up, rwkv_k_k, rwkv_k_a, rwkv_r_k, rwkv_ln_w, rwkv_ln_b, rwkv_w_out):
    B, S, D = x.shape
    depth = norm_pre.shape[0]
    n_heads = fox_b_f.shape[1]
    L0 = N_META + S
    L = -(-L0 // BLK) * BLK
    assert (B * L) % ROW_TILE == 0 and D % LANES == 0
    meta = jnp.broadcast_to(meta_tokens[None].astype(x.dtype), (B, N_META, D))
    h = jnp.concatenate([meta, x, jnp.zeros((B, L - L0, D), x.dtype)], axis=1).reshape(B * L, D)

    def in_proj_args(i):
        g = norm_pre[i].reshape(1, D)
        if i % 2 == 0:
            wf = fox_w_in[i // 2]
            w = jnp.concatenate([wf[:, :2 * D], wf[:, 3 * D:], jnp.zeros((D, LANES - n_heads), F32)],
                                axis=1)
            return ((w.astype(BF16), wf[:, 2 * D:3 * D].T.astype(BF16)), g,
                    dict(q_cols=D, q_scale=LOG2E * HEAD ** -0.5))
        return (rwkv_w_in[i // 2].astype(BF16),), g, {}

    ws, g, qkw = in_proj_args(0)
    p, tail, *v_t = _norm_proj(h, g, *ws, **qkw)
    for i in range(depth):
        j = i // 2
        if i % 2 == 0:
            z = _fox_mixer(p, tail, v_t[0], B, L, fox_b_f[j])
            w_out = fox_w_out[j]
        else:
            z = _rwkv_mixer(p, tail, B, L, L0, rwkv_mu[j], rwkv_w0[j], rwkv_w_up[j], rwkv_a0[j],
                            rwkv_a_up[j], rwkv_k_k[j], rwkv_k_a[j], rwkv_r_k[j], rwkv_ln_w[j],
                            rwkv_ln_b[j])
            w_out = rwkv_w_out[j]
        g_post = norm_post[i].reshape(1, D)
        if i + 1 < depth:
            ws, g, qkw = in_proj_args(i + 1)
            h, p, tail, *v_t = _out_norm_proj(z, w_out.astype(BF16), h, g_post, g, *ws, **qkw)
        elif S % ROW_TILE == 0:
            return _out_proj_final(z, w_out.astype(BF16), h, g_post, B, L, S).reshape(B, S, D)
        else:
            h = _out_proj(z, w_out.astype(BF16), h, g_post)
    return h.reshape(B, L, D)[:, N_META:L0]
```

```python
import functools
import math

import jax
import jax.numpy as jnp
from jax import lax
from jax.experimental import pallas as pl
from jax.experimental.pallas import tpu as pltpu

F32 = jnp.float32
BF16 = jnp.bfloat16

N_META = 16
HEAD = 64
LANES = 128
BLK = 128
ATTN_PAIRS = 8
ATTN_SUBGROUP = 8
KV_GROUP = 4
ONES_LANE = 120
SUM_ROWS = 16
CHUNK = 64
WKV_BATCH = 8
ROW_TILE = 512
COL_TILE = 512
NORM_EPS = 1e-6
GN_EPS = 64e-5
DECAY_SCALE = math.exp(-0.5)
LOG2E = math.log2(math.e)
NEG = -1e30
VMEM_LIMIT = 48 * 1024 * 1024
VMEM_LIMIT_FUSED = 56 * 1024 * 1024


def _dot(a, b):
    return jnp.dot(a, b, preferred_element_type=F32)


def _dot_nt(a, b):
    return lax.dot_general(a, b, (((1,), (1,)), ((), ())), preferred_element_type=F32)


def _dot_tn(a, b):
    return lax.dot_general(a, b, (((0,), (0,)), ((), ())), preferred_element_type=F32)


def _split3(x):
    hi = x.astype(BF16)
    r1 = x - hi.astype(F32)
    mid = r1.astype(BF16)
    lo = (r1 - mid.astype(F32)).astype(BF16)
    return hi, mid, lo


def _sigmoid(x):
    return 0.5 * jnp.tanh(0.5 * x) + 0.5


def _project(x, g_ref, w_refs, outs, n_main, q_cols, q_scale):
    var = jnp.mean(x * x, axis=-1, keepdims=True)
    u = (x * lax.rsqrt(var + NORM_EPS) * g_ref[...]).astype(BF16)
    w_ref, (o_ref, t_ref) = w_refs[0], outs[:2]
    for j in range(0, n_main, COL_TILE):
        acc = _dot(u, w_ref[:, j:j + COL_TILE])
        if j < q_cols:
            acc = acc * q_scale
        o_ref[:, j:j + COL_TILE] = acc.astype(BF16)
    t_ref[...] = _dot(u, w_ref[:, n_main:])
    if len(w_refs) > 1:
        for j in range(0, w_refs[1].shape[0], COL_TILE):
            acc_t = _dot_nt(w_refs[1][j:j + COL_TILE, :], u).astype(BF16)
            for t in range(x.shape[0] // BLK):
                outs[2][t, j:j + COL_TILE, :] = acc_t[:, t * BLK:(t + 1) * BLK]


def _proj_specs(T, w, w_t):
    n_main = w.shape[1] - LANES
    whole = lambda a: pl.BlockSpec(a.shape, lambda i: (0,) * a.ndim)
    in_specs = [whole(w)]
    out_specs = [pl.BlockSpec((ROW_TILE, n_main), lambda i: (i, 0)),
                 pl.BlockSpec((ROW_TILE, LANES), lambda i: (i, 0))]
    shapes = [jax.ShapeDtypeStruct((T, n_main), BF16), jax.ShapeDtypeStruct((T, LANES), F32)]
    if w_t is not None:
        in_specs.append(whole(w_t))
        out_specs.append(pl.BlockSpec((ROW_TILE // BLK, w_t.shape[0], BLK), lambda i: (i, 0, 0)))
        shapes.append(jax.ShapeDtypeStruct((T // BLK, w_t.shape[0], BLK), BF16))
    return in_specs, out_specs, shapes


def _norm_proj_kernel(h_ref, g_ref, *refs, n_w, n_main, q_cols, q_scale):
    _project(h_ref[...], g_ref, refs[:n_w], refs[n_w:], n_main, q_cols, q_scale)


def _norm_proj(h, g, w, w_t=None, *, q_cols=0, q_scale=1.0):
    T, D = h.shape
    ws = [w] if w_t is None else [w, w_t]
    w_specs, out_specs, out_shape = _proj_specs(T, w, w_t)
    kern = functools.partial(_norm_proj_kernel, n_w=len(ws), n_main=w.shape[1] - LANES,
                             q_cols=q_cols, q_scale=q_scale)
    return pl.pallas_call(
        kern,
        grid=(T // ROW_TILE,),
        in_specs=[pl.BlockSpec((ROW_TILE, D), lambda i: (i, 0)),
                  pl.BlockSpec((1, D), lambda i: (0, 0))] + w_specs,
        out_specs=out_specs,
        out_shape=out_shape,
        compiler_params=pltpu.CompilerParams(
            dimension_semantics=("arbitrary",), vmem_limit_bytes=VMEM_LIMIT),
        name="norm_proj",
    )(h, g, *ws)


def _residual(z_ref, w_ref, h_ref, g_ref):
    m = _dot(z_ref[...], w_ref[...])
    var = jnp.mean(m * m, axis=-1, keepdims=True)
    return h_ref[...] + m * lax.rsqrt(var + NORM_EPS) * g_ref[...]


def _out_proj_kernel(z_ref, w_ref, h_ref, g_ref, o_ref):
    o_ref[...] = _residual(z_ref, w_ref, h_ref, g_ref)


def _out_norm_proj_kernel(z_ref, wo_ref, h_ref, gpost_ref, gpre_ref, *refs,
                          n_w, n_main, q_cols, q_scale):
    hn = _residual(z_ref, wo_ref, h_ref, gpost_ref)
    refs[n_w][...] = hn
    _project(hn, gpre_ref, refs[:n_w], refs[n_w + 1:], n_main, q_cols, q_scale)


def _out_norm_proj(z, w_out, h, g_post, g_pre, w, w_t=None, *, q_cols=0, q_scale=1.0):
    T, D = h.shape
    ws = [w] if w_t is None else [w, w_t]
    w_specs, out_specs, out_shape = _proj_specs(T, w, w_t)
    kern = functools.partial(_out_norm_proj_kernel, n_w=len(ws), n_main=w.shape[1] - LANES,
                             q_cols=q_cols, q_scale=q_scale)
    rows = lambda n: pl.BlockSpec((ROW_TILE, n), lambda i: (i, 0))
    whole = lambda a: pl.BlockSpec(a.shape, lambda i: (0, 0))
    return pl.pallas_call(
        kern,
        grid=(T // ROW_TILE,),
        in_specs=[rows(D), whole(w_out), rows(D), whole(g_post), whole(g_pre)] + w_specs,
        out_specs=[rows(D)] + out_specs,
        out_shape=[jax.ShapeDtypeStruct((T, D), F32)] + out_shape,
        compiler_params=pltpu.CompilerParams(
            dimension_semantics=("arbitrary",), vmem_limit_bytes=VMEM_LIMIT_FUSED),
        name="out_norm_proj",
    )(z, w_out, h, g_post, g_pre, *ws)


def _out_proj(z, w, h, g):
    T, D = h.shape
    return pl.pallas_call(
        _out_proj_kernel,
        grid=(T // ROW_TILE,),
        in_specs=[
            pl.BlockSpec((ROW_TILE, D), lambda i: (i, 0)),
            pl.BlockSpec((D, D), lambda i: (0, 0)),
            pl.BlockSpec((ROW_TILE, D), lambda i: (i, 0)),
            pl.BlockSpec((1, D), lambda i: (0, 0)),
        ],
        out_specs=pl.BlockSpec((ROW_TILE, D), lambda i: (i, 0)),
        out_shape=jax.ShapeDtypeStruct((T, D), F32),
        compiler_params=pltpu.CompilerParams(
            dimension_semantics=("arbitrary",), vmem_limit_bytes=VMEM_LIMIT),
        name="out_proj",
    )(z, w, h, g)


def _out_proj_final(z, w, h, g, B, L, S):
    D = h.shape[1]
    n_r = S // ROW_TILE
    rows_in = pl.BlockSpec((pl.Element(ROW_TILE), pl.Element(D)),
                           lambda b, r: (pl.multiple_of(b * L + N_META + r * ROW_TILE, 8), 0))
    return pl.pallas_call(
        _out_proj_kernel,
        grid=(B, n_r),
        in_specs=[rows_in, pl.BlockSpec((D, D), lambda b, r: (0, 0)), rows_in,
                  pl.BlockSpec((1, D), lambda b, r: (0, 0))],
        out_specs=pl.BlockSpec((ROW_TILE, D), lambda b, r: (b * n_r + r, 0)),
        out_shape=jax.ShapeDtypeStruct((B * S, D), F32),
        compiler_params=pltpu.CompilerParams(
            dimension_semantics=("arbitrary", "arbitrary"), vmem_limit_bytes=VMEM_LIMIT),
        name="out_proj_final",
    )(z, w, h, g)


def _fox_bias_kernel(t_ref, b_ref, x_ref, q_ref):
    L = t_ref.shape[1]
    n_heads = q_ref.shape[2]
    assert 3 * n_heads <= ONES_LANE
    x = t_ref[0] + b_ref[...]
    lf = jnp.minimum(x, 0.0) - jnp.log(1.0 + jnp.exp(-jnp.abs(x)))
    row = lax.broadcasted_iota(jnp.int32, (BLK, BLK), 0)
    col = lax.broadcasted_iota(jnp.int32, (BLK, BLK), 1)
    tril = (col <= row).astype(BF16)
    carry = jnp.zeros((1, LANES), F32)
    blocks = []
    for j in range(L // BLK):
        hi, mid, lo = _split3(lf[j * BLK:(j + 1) * BLK])
        blk = _dot(tril, hi) + _dot(tril, mid) + _dot(tril, lo) + carry
        carry = blk[BLK - 1:BLK]
        blocks.append(blk)
    cum2 = jnp.concatenate(blocks, axis=0) * LOG2E

    ext = jnp.zeros((L, LANES), F32)
    for o, term in enumerate(_split3(-cum2)):
        ext = ext + _dot(term, ((row < n_heads) & (col == 3 * row + o)).astype(BF16))
    lane_w = lax.broadcasted_iota(jnp.int32, (1, LANES), 1)
    x_ref[0] = jnp.where((lane_w >= ONES_LANE) & (lane_w < ONES_LANE + 3), 1.0, ext).astype(BF16)

    hrow = lax.broadcasted_iota(jnp.int32, (n_heads, LANES), 0)
    lane = lax.broadcasted_iota(jnp.int32, (n_heads, LANES), 1)
    own = ((lane >= 3 * hrow) & (lane < 3 * hrow + 3)).astype(F32)
    for j in range(L // BLK):
        diag = jnp.where(hrow == lane, blocks[j][0:1] * LOG2E, 0.0)
        rows = own
        for o, term in enumerate(_split3(diag)):
            rows = rows + _dot(term, (col == ONES_LANE + o).astype(BF16))
        q_ref[0, j] = rows


def _fox_bias(tail3, b_row, n_heads):
    B, L, _ = tail3.shape
    nq = L // BLK
    return pl.pallas_call(
        _fox_bias_kernel,
        grid=(B,),
        in_specs=[pl.BlockSpec((1, L, LANES), lambda b: (b, 0, 0)),
                  pl.BlockSpec((1, LANES), lambda b: (0, 0))],
        out_specs=[pl.BlockSpec((1, L, LANES), lambda b: (b, 0, 0)),
                   pl.BlockSpec((1, nq, n_heads, LANES), lambda b: (b, 0, 0, 0))],
        out_shape=[jax.ShapeDtypeStruct((B, L, LANES), BF16),
                   jax.ShapeDtypeStruct((B, nq, n_heads, LANES), F32)],
        compiler_params=pltpu.CompilerParams(
            dimension_semantics=("arbitrary",), vmem_limit_bytes=VMEM_LIMIT),
        name="fox_bias",
    )(tail3, b_row)


def _fox_attn_kernel(q_ref, k_ref, x_ref, vt_ref, g_ref, qx_ref, o_ref):
    i = pl.program_id(2)
    n_p = q_ref.shape[2] // LANES
    pairs = range(n_p)
    ps = [slice(p * LANES, (p + 1) * LANES) for p in pairs]
    lane = lax.broadcasted_iota(jnp.int32, (1, LANES), 1)
    is_a = lane < HEAD
    zq = jnp.zeros((), BF16)
    ones_rows = jnp.ones((SUM_ROWS, BLK), BF16)

    def ext_rows(h):
        return jnp.broadcast_to(qx_ref[0, 0, h:h + 1, :], (BLK, LANES)).astype(BF16)

    qx = []
    for p in pairs:
        q2 = q_ref[0, :, ps[p]]
        qx.append(jnp.concatenate(
            [jnp.concatenate([jnp.where(is_a, q2, zq), ext_rows(2 * p)], axis=1),
             jnp.concatenate([jnp.where(is_a, zq, q2), ext_rows(2 * p + 1)], axis=1)], axis=0))

    key_i = lax.broadcasted_iota(jnp.int32, (BLK, 2 * BLK), 0)
    qry_i = lax.broadcasted_iota(jnp.int32, (BLK, 2 * BLK), 1) % BLK
    keep = key_i <= qry_i

    def step(j, carry, nb=1):
        m, acc = carry
        off = pl.multiple_of(j * BLK, BLK)
        unmasked = jnp.logical_or(keep, j + (nb - 1) != i)
        m, acc = list(m), list(acc)
        n_sub = ATTN_SUBGROUP if nb > 1 else n_p
        for g0 in range(0, n_p, n_sub):
            sub = range(g0, min(g0 + n_sub, n_p))
            st = {}
            for p in sub:
                kx = jnp.concatenate([k_ref[0, pl.ds(off, nb * BLK), ps[p]],
                                      x_ref[0, pl.ds(off, nb * BLK), :]], axis=1)
                s = _dot_nt(kx, qx[p])
                last = jnp.where(unmasked, s[(nb - 1) * BLK:], NEG)
                st[p] = last if nb == 1 else jnp.concatenate([s[:(nb - 1) * BLK], last], axis=0)
            for p in sub:
                mn = jnp.maximum(m[p], jnp.max(st[p], axis=0, keepdims=True))
                alpha = jnp.exp2(m[p] - mn)
                pr = jnp.exp2(st[p] - mn).astype(BF16)
                vt = jnp.concatenate(
                    [jnp.concatenate([vt_ref[0, j + t, ps[p], :], ones_rows], axis=0)
                     for t in range(nb)], axis=1)
                acc[p] = acc[p] * alpha + _dot(vt, pr)
                m[p] = mn
        return tuple(m), tuple(acc)

    n_grp = (i + 1) // KV_GROUP
    carry = (tuple(jnp.full((1, 2 * BLK), NEG, F32) for _ in pairs),
             tuple(jnp.zeros((LANES + SUM_ROWS, 2 * BLK), F32) for _ in pairs))
    carry = lax.fori_loop(0, n_grp, lambda jj, c: step(jj * KV_GROUP, c, KV_GROUP), carry)
    _, acc = lax.fori_loop(n_grp * KV_GROUP, i + 1, step, carry)
    for p in pairs:
        o_t = acc[p][:LANES] / acc[p][LANES:LANES + 1]
        o = jnp.concatenate([o_t[:HEAD, :BLK], o_t[HEAD:LANES, BLK:]], axis=0).T
        g = g_ref[0, :, ps[p]].astype(F32)
        o_ref[0, :, ps[p]] = (o * (g * _sigmoid(g))).astype(BF16)


def _fox_attn(p3, kext, v_t, qrows):
    B, L, _ = p3.shape
    n_pair = (p3.shape[2] // 3) // LANES
    nq = L // BLK
    n_p = ATTN_PAIRS
    width = n_p * LANES
    n_grp = n_pair // n_p
    assert n_grp == 1 and qrows.shape[2] == 2 * n_pair
    return pl.pallas_call(
        _fox_attn_kernel,
        grid=(B, n_grp, nq),
        in_specs=[
            pl.BlockSpec((1, BLK, width), lambda b, p, i: (b, i, p)),
            pl.BlockSpec((1, L, width), lambda b, p, i: (b, 0, n_grp + p)),
            pl.BlockSpec((1, L, LANES), lambda b, p, i: (b, 0, 0)),
            pl.BlockSpec((1, nq, width, BLK), lambda b, p, i: (b, 0, p, 0)),
            pl.BlockSpec((1, BLK, width), lambda b, p, i: (b, i, 2 * n_grp + p)),
            pl.BlockSpec((1, 1, 2 * n_pair, LANES), lambda b, p, i: (b, i, 0, 0)),
        ],
        out_specs=pl.BlockSpec((1, BLK, width), lambda b, p, i: (b, i, p)),
        out_shape=jax.ShapeDtypeStruct((B, L, n_pair * LANES), BF16),
        compiler_params=pltpu.CompilerParams(
            dimension_semantics=("arbitrary", "arbitrary", "arbitrary"), vmem_limit_bytes=VMEM_LIMIT),
        name="fox_attn",
    )(p3, p3, kext, v_t, p3, qrows)


def _head_sums(x):
    row = lax.broadcasted_iota(jnp.int32, (LANES, LANES), 0)
    col = lax.broadcasted_iota(jnp.int32, (LANES, LANES), 1)
    ones = ((row // HEAD) == (col // HEAD)).astype(BF16)
    n_g, rows = x.shape[1] // LANES, x.shape[0]
    xs = jnp.concatenate([x[:, g * LANES:(g + 1) * LANES] for g in range(n_g)], axis=0).astype(BF16)
    s = _dot(xs, ones)
    return jnp.concatenate([s[g * rows:(g + 1) * rows] for g in range(n_g)], axis=1)


def _wkv_kernel(r_ref, k_ref, v_ref, g_ref, rh_ref, kh_ref, vh_ref, gh_ref, t_ref, th_ref,
                mu_ref, mut_ref, w0_ref, wup_ref, a0_ref, aup_ref, kk_ref, ka_ref, rk_ref,
                lnw_ref, lnb_ref, o_ref, st_ref, *, n_real):
    c = pl.program_id(1)

    @pl.when(c == 0)
    def _():
        st_ref[...] = jnp.zeros_like(st_ref)

    @pl.when(c >= n_real)
    def _():
        o_ref[...] = jnp.zeros_like(o_ref)

    @pl.when(c < n_real)
    def _():
        _wkv_chunk(r_ref, k_ref, v_ref, g_ref, rh_ref, kh_ref, vh_ref, gh_ref, t_ref, th_ref,
                   mu_ref, mut_ref, w0_ref, wup_ref, a0_ref, aup_ref, kk_ref, ka_ref, rk_ref,
                   lnw_ref, lnb_ref, o_ref, st_ref, c)


def _wkv_chunk(r_ref, k_ref, v_ref, g_ref, rh_ref, kh_ref, vh_ref, gh_ref, t_ref, th_ref,
               mu_ref, mut_ref, w0_ref, wup_ref, a0_ref, aup_ref, kk_ref, ka_ref, rk_ref,
               lnw_ref, lnb_ref, o_ref, st_ref, c):
    n_b = r_ref.shape[0]
    D = r_ref.shape[2]
    n_pair = D // LANES
    not_first = c != 0
    row = lax.broadcasted_iota(jnp.int32, (CHUNK, CHUNK), 0)
    col = lax.broadcasted_iota(jnp.int32, (CHUNK, CHUNK), 1)
    sub_diag = (col + 1 == row).astype(BF16)
    tril = (col <= row).astype(BF16)
    top0 = (lax.broadcasted_iota(jnp.int32, (8, 1), 0) == 0) & not_first
    trow0 = lax.broadcasted_iota(jnp.int32, (CHUNK, 1), 0) == 0

    def lerp(cur, sh, prev, mu):
        sh = jnp.concatenate([sh[0:8] + jnp.where(top0, prev, 0.0), sh[8:]], axis=0)
        return cur + (sh - cur) * mu

    def shifted(cur, prev, mu):
        return lerp(cur.astype(F32), _dot(sub_diag, cur), prev.astype(F32), mu)

    tls = []
    for b in range(n_b):
        tcur = t_ref[b]
        tls.append(lerp(tcur, jnp.where(trow0, 0.0, pltpu.roll(tcur, 1, axis=0)), th_ref[b, 7:8, :],
                        mut_ref[...]))
    tl = jnp.concatenate(tls, axis=0)
    w_log_all = w0_ref[...] + _dot(jnp.tanh(tl[:, :HEAD]).astype(BF16), wup_ref[...])
    a_all = _sigmoid(a0_ref[...] + _dot(tl[:, HEAD:].astype(BF16), aup_ref[...]))

    def prep(b):
        r = shifted(r_ref[b], rh_ref[b, 7:8, :], mu_ref[:, 0:D])
        k = shifted(k_ref[b], kh_ref[b, 7:8, :], mu_ref[:, D:2 * D])
        v = shifted(v_ref[b], vh_ref[b, 7:8, :], mu_ref[:, 2 * D:3 * D])
        g = shifted(g_ref[b], gh_ref[b, 7:8, :], mu_ref[:, 3 * D:4 * D])
        w_log = w_log_all[b * CHUNK:(b + 1) * CHUNK]
        a = a_all[b * CHUNK:(b + 1) * CHUNK]
        lw = (-0.5 * DECAY_SCALE * LOG2E) * jnp.tanh(0.5 * w_log) - 0.5 * DECAY_SCALE * LOG2E
        kk = k * kk_ref[...]
        kk = kk * lax.rsqrt(jnp.maximum(_head_sums(kk * kk), 1e-24))
        k = k * (1.0 + (a - 1.0) * ka_ref[...])
        hi, mid, lo = _split3(lw)
        cl = _dot(tril, hi) + _dot(tril, mid) + _dot(tril, lo)
        e_neg = jnp.exp2(-cl)
        return dict(
            r=r, k=k, v=v, g=g,
            r_t=(r * jnp.exp2(cl)).astype(BF16),
            a_t=(-kk * jnp.exp2(cl - lw)).astype(BF16),
            b_t=(kk * a * e_neg).astype(BF16),
            k_t=(k * e_neg).astype(BF16),
            w_end=jnp.exp2(cl[CHUNK - 1:CHUNK, :]),
            v_b=v.astype(BF16))

    pre = [prep(b) for b in range(n_b)]

    lane = lax.broadcasted_iota(jnp.int32, (1, LANES), 1)
    is_a = lane < HEAD
    trw = lax.broadcasted_iota(jnp.int32, (CHUNK, LANES), 0)
    jw = lax.broadcasted_iota(jnp.int32, (CHUNK, LANES), 1) % HEAD
    strict = jw < trw
    incl = jw <= trw
    r2 = lax.broadcasted_iota(jnp.int32, (LANES, LANES), 0) // HEAD
    c2 = lax.broadcasted_iota(jnp.int32, (LANES, LANES), 1) // HEAD
    same_head = r2 == c2
    zb16 = jnp.zeros((), BF16)

    def bdiag(xw, swap=False):
        za = jnp.where(is_a, xw, jnp.zeros_like(xw))
        zb = jnp.where(is_a, jnp.zeros_like(xw), xw)
        return jnp.concatenate([zb, za] if swap else [za, zb], axis=0)

    streams = [(b, p) for b in range(n_b) for p in range(n_pair)]
    ns = range(len(streams))

    def part(name, n):
        b, p = streams[n]
        return pre[b][name][:, p * LANES:(p + 1) * LANES]

    xq = [jnp.concatenate([part("a_t", n), part("r_t", n)], axis=0) for n in ns]
    ycat = [jnp.concatenate(
        [jnp.where(is_a, jnp.concatenate([part("k_t", n), part("b_t", n)], axis=0), zb16),
         jnp.where(is_a, zb16, jnp.concatenate([part("b_t", n), part("k_t", n)], axis=0))],
        axis=0) for n in ns]
    sc = [_dot_nt(xq[n], ycat[n]) for n in ns]
    s0 = [st_ref[b, p] for b, p in streams]
    xs = [_dot_nt(xq[n], s0[n].astype(BF16)) for n in ns]
    top_lo = [sc[n][:CHUNK, :LANES] for n in ns]
    top_hi = [sc[n][:CHUNK, LANES:] for n in ns]
    bot_lo = [sc[n][CHUNK:, :LANES] for n in ns]
    bot_hi = [sc[n][CHUNK:, LANES:] for n in ns]
    a_ak = [jnp.where(strict, jnp.where(is_a, top_lo[n], top_hi[n]), 0.0).astype(BF16) for n in ns]
    a_ab = [jnp.where(strict, jnp.where(is_a, top_hi[n], top_lo[n]), 0.0) for n in ns]
    m_rk = [jnp.where(incl, jnp.where(is_a, bot_lo[n], bot_hi[n]), 0.0).astype(BF16) for n in ns]
    m_rb = [jnp.where(incl, jnp.where(is_a, bot_hi[n], bot_lo[n]), 0.0).astype(BF16) for n in ns]

    eye = (jw == trw).astype(F32)
    lvl1 = (trw // 2 == jw // 2) & strict
    tw = [eye + jnp.where(lvl1, a_ab[n], 0.0) for n in ns]
    s = 2
    while s < CHUNK:
        off = (trw // (2 * s) == jw // (2 * s)) & ((trw // s) % 2 == 1) & ((jw // s) % 2 == 0)
        tb = [t.astype(BF16) for t in tw]
        pm = [_dot(tb[n], bdiag(jnp.where(off, a_ab[n], 0.0).astype(BF16))) for n in ns]
        tw = [tw[n] + _dot(pm[n].astype(BF16), bdiag(tb[n])) for n in ns]
        s *= 2

    vbd = [bdiag(part("v_b", n)) for n in ns]
    gm = [xs[n][:CHUNK] + _dot(a_ak[n], vbd[n]) for n in ns]
    u = [_dot(tw[n].astype(BF16), bdiag(gm[n].astype(BF16), swap=True)).astype(BF16) for n in ns]
    ys = [xs[n][CHUNK:] + _dot(jnp.concatenate([m_rb[n], m_rk[n]], axis=1),
                               jnp.concatenate([bdiag(u[n], swap=True), vbd[n]], axis=0))
          for n in ns]
    for n in ns:
        b, p = streams[n]
        uv = jnp.concatenate([u[n], part("v_b", n)], axis=0)
        bk = jnp.concatenate([part("b_t", n), part("k_t", n)], axis=0)
        st_ref[b, p] = (s0[n] + jnp.where(same_head, _dot_tn(uv, bk), 0.0)) * part("w_end", n)

    inv_n = 1.0 / HEAD
    for b in range(n_b):
        y = jnp.concatenate(ys[b * n_pair:(b + 1) * n_pair], axis=1)
        q = pre[b]
        mean = _head_sums(y) * inv_n
        d = y - mean
        var = _head_sums(d * d) * inv_n
        yn = d * lax.rsqrt(var + GN_EPS) * lnw_ref[...] + lnb_ref[...]
        bonus = _head_sums(q["r"] * q["k"] * rk_ref[...]) * q["v"]
        o_ref[b] = ((yn + bonus) * (q["g"] * _sigmoid(q["g"]))).astype(BF16)


def _wkv(p3, tail3, n_real, mu, mu_t, w0, w_up, a0, a_up, k_k, k_a, r_k, ln_w, ln_b):
    B, L, n_main = p3.shape
    D = n_main // 4
    nc = L // CHUNK
    sub = CHUNK // 8
    n_b = WKV_BATCH if B % WKV_BATCH == 0 else 1

    def cur(s):
        return pl.BlockSpec((n_b, CHUNK, D), lambda b, c: (b, c, s))

    def halo(s):
        return pl.BlockSpec((n_b, 8, D), lambda b, c: (b, jnp.maximum(c * sub - 1, 0), s))

    def vec(n):
        return pl.BlockSpec((1, n), lambda b, c: (0, 0))

    lora = pl.BlockSpec((HEAD, D), lambda b, c: (0, 0))
    return pl.pallas_call(
        functools.partial(_wkv_kernel, n_real=n_real),
        grid=(B // n_b, nc),
        in_specs=[cur(0), cur(1), cur(2), cur(3), halo(0), halo(1), halo(2), halo(3),
                  pl.BlockSpec((n_b, CHUNK, LANES), lambda b, c: (b, c, 0)),
                  pl.BlockSpec((n_b, 8, LANES), lambda b, c: (b, jnp.maximum(c * sub - 1, 0), 0)),
                  vec(n_main), vec(LANES), vec(D), lora, vec(D), lora,
                  vec(D), vec(D), vec(D), vec(D), vec(D)],
        out_specs=pl.BlockSpec((n_b, CHUNK, D), lambda b, c: (b, c, 0)),
        out_shape=jax.ShapeDtypeStruct((B, L, D), BF16),
        scratch_shapes=[pltpu.VMEM((n_b, D // LANES, LANES, LANES), F32)],
        compiler_params=pltpu.CompilerParams(
            dimension_semantics=("arbitrary", "arbitrary"), vmem_limit_bytes=VMEM_LIMIT),
        name="wkv7",
    )(p3, p3, p3, p3, p3, p3, p3, p3, tail3, tail3,
      mu, mu_t, w0, w_up, a0, a_up, k_k, k_a, r_k, ln_w, ln_b)


def _fox_mixer(p, tail, v_t, B, L, b_f):
    D = p.shape[1] // 3
    n_heads = b_f.shape[0]
    b_row = jnp.pad(b_f, (0, LANES - n_heads)).reshape(1, LANES)
    kext, qrows = _fox_bias(tail.reshape(B, L, LANES), b_row, n_heads)
    return _fox_attn(p.reshape(B, L, 3 * D), kext, v_t.reshape(B, L // BLK, D, BLK),
                     qrows).reshape(B * L, D)


def _rwkv_mixer(p, tail, B, L, L0, mu, w0, w_up, a0, a_up, k_k, k_a, r_k, ln_w, ln_b):
    n_main = p.shape[1]
    row = lambda t: t.reshape(1, -1)
    z = _wkv(p.reshape(B, L, n_main), tail.reshape(B, L, LANES), -(-L0 // CHUNK),
             row(mu[:n_main]), row(mu[n_main:]), row(w0), w_up.astype(BF16), row(a0),
             a_up.astype(BF16), row(k_k), row(k_a), row(r_k), row(ln_w), row(ln_b))
    return z.reshape(B * L, n_main // 4)


def kernel(x, meta_tokens, norm_pre, norm_post, fox_w_in, fox_b_f, fox_w_out, rwkv_w_in, rwkv_mu, rwkv_w0, rwkv_w_up, rwkv_a0, rwkv_a_up, rwkv_k_k, rwkv_k_a, rwkv_r_k, rwkv_ln_w, rwkv_ln_b, rwkv_w_out):
    B, S, D = x.shape
    depth = norm_pre.shape[0]
    n_heads = fox_b_f.shape[1]
    L0 = N_META + S
    L = -(-L0 // BLK) * BLK
    assert (B * L) % ROW_TILE == 0 and D % LANES == 0
    meta = jnp.broadcast_to(meta_tokens[None].astype(x.dtype), (B, N_META, D))
    h = jnp.concatenate([meta, x, jnp.zeros((B, L - L0, D), x.dtype)], axis=1).reshape(B * L, D)

    def in_proj_args(i):
        g = norm_pre[i].reshape(1, D)
        if i % 2 == 0:
            wf = fox_w_in[i // 2]
            w = jnp.concatenate([wf[:, :2 * D], wf[:, 3 * D:], jnp.zeros((D, LANES - n_heads), F32)],
                                axis=1)
            return ((w.astype(BF16), wf[:, 2 * D:3 * D].T.astype(BF16)), g,
                    dict(q_cols=D, q_scale=LOG2E * HEAD ** -0.5))
        return (rwkv_w_in[i // 2].astype(BF16),), g, {}

    ws, g, qkw = in_proj_args(0)
    p, tail, *v_t = _norm_proj(h, g, *ws, **qkw)
    for i in range(depth):
        j = i // 2
        if i % 2 == 0:
            z = _fox_mixer(p, tail, v_t[0], B, L, fox_b_f[j])
            w_out = fox_w_out[j]
        else:
            z = _rwkv_mixer(p, tail, B, L, L0, rwkv_mu[j], rwkv_w0[j], rwkv_w_up[j], rwkv_a0[j],
                            rwkv_a_up[j], rwkv_k_k[j], rwkv_k_a[j], rwkv_r_k[j], rwkv_ln_w[j],
                            rwkv_ln_b[j])
            w_out = rwkv_w_out[j]
        g_post = norm_post[i].reshape(1, D)
        if i + 1 < depth:
            ws, g, qkw = in_proj_args(i + 1)
            h, p, tail, *v_t = _out_norm_proj(z, w_out.astype(BF16), h, g_post, g, *ws, **qkw)
        elif S % ROW_TILE == 0:
            return _out_proj_final(z, w_out.astype(BF16), h, g_post, B, L, S).reshape(B, S, D)
        else:
            h = _out_proj(z, w_out.astype(BF16), h, g_post)
    return h.reshape(B, L, D)[:, N_META:L0]
```
